```python
import math
import jax, jax.numpy as jnp
from jax import lax
import numpy as np

D_MODEL = 2048
BATCH = 8
SEQ = 2048
DEPTH = 2

MIX_WIDTH = D_MODEL
CONV_WIDTH = D_MODEL // 4
CONV_GROUPS = 4
CONV_K = 3
SPARSE_HEADS = 6
SPARSE_HEAD_DIM = 128
SPARSE_WIDTH = SPARSE_HEADS * SPARSE_HEAD_DIM
IDX_HEADS = 8
IDX_DIM = 64
INDEX_TOPK_MAX = 256
DIFF_HEADS = 6
DIFF_QK_DIM = 64
DIFF_V_DIM = 2 * DIFF_QK_DIM
DIFF_WIDTH = DIFF_HEADS * DIFF_V_DIM

ROPE_THETA = 10000.0
NORM_EPS = 1e-6
SUBLN_EPS = 1e-5
Q_BLOCK = 128

IN_SPLITS = (CONV_WIDTH, CONV_WIDTH, CONV_WIDTH, CONV_WIDTH,
             SPARSE_WIDTH, SPARSE_HEAD_DIM, SPARSE_HEAD_DIM, IDX_HEADS * IDX_DIM, IDX_DIM, IDX_HEADS, SPARSE_WIDTH,
             2 * DIFF_HEADS * DIFF_QK_DIM, 2 * DIFF_HEADS * DIFF_QK_DIM, DIFF_WIDTH, DIFF_WIDTH)
N_IN = sum(IN_SPLITS)

kernel_name = "hybrid_conv_dsa_diffattn_parallel_heads"


def _rmsnorm(x, g, eps):
    xf = x.astype(jnp.float32)
    y = xf * lax.rsqrt(jnp.mean(xf * xf, axis=-1, keepdims=True) + eps)
    return (y * g.astype(jnp.float32)).astype(x.dtype)


def _rope(x, pos):
    d = x.shape[-1]
    half = d // 2
    inv_freq = jnp.exp(-math.log(ROPE_THETA) * jnp.arange(half, dtype=jnp.float32) * (2.0 / d))
    ang = pos.astype(jnp.float32)[..., None] * inv_freq
    cos = jnp.cos(ang)[:, :, None, :]
    sin = jnp.sin(ang)[:, :, None, :]
    xf = x.astype(jnp.float32)
    x1, x2 = xf[..., :half], xf[..., half:]
    return jnp.concatenate([x1 * cos - x2 * sin, x1 * sin + x2 * cos], axis=-1).astype(x.dtype)


def _split_cols(u):
    offs = np.cumsum(np.array(IN_SPLITS))[:-1].tolist()
    return jnp.split(u, offs, axis=-1)


def _short_conv_mixer(b_gate, c_gate, h, conv_w):
    z = c_gate * h
    s = z.shape[1]
    zp = jnp.pad(z, ((0, 0), (CONV_K - 1, 0), (0, 0)))
    conv = sum(conv_w[j] * zp[:, j:j + s] for j in range(CONV_K))
    return b_gate * conv


def _dsa_mixer(q, k, v, qi, ki, wi, pos):
    bsz, s, _ = q.shape
    q = _rope(q.reshape(bsz, s, SPARSE_HEADS, SPARSE_HEAD_DIM), pos)
    k = _rope(k[:, :, None, :], pos)[:, :, 0]
    qi = _rope(qi.reshape(bsz, s, IDX_HEADS, IDX_DIM), pos)
    ki = _rope(ki[:, :, None, :], pos)[:, :, 0].astype(jnp.float32)
    top_k = min(INDEX_TOPK_MAX, s // 4)
    n_blk = s // Q_BLOCK
    key_pos = jnp.arange(s)
    idx_scale = (IDX_DIM * IDX_HEADS) ** -0.5
    att_scale = SPARSE_HEAD_DIM ** -0.5
    gather = jax.vmap(lambda arr, ids: arr[ids])

    def block(i):
        start = i * Q_BLOCK
        qb = lax.dynamic_slice_in_dim(q, start, Q_BLOCK, axis=1)
        qib = lax.dynamic_slice_in_dim(qi, start, Q_BLOCK, axis=1)
        wib = lax.dynamic_slice_in_dim(wi, start, Q_BLOCK, axis=1)
        qpos = start + jnp.arange(Q_BLOCK)
        logits = jnp.einsum('bqhd,bsd->bqhs', qib.astype(jnp.float32), ki)
        score = jnp.einsum('bqh,bqhs->bqs', wib.astype(jnp.float32), jax.nn.relu(logits)) * idx_scale
        causal = key_pos[None, :] <= qpos[:, None]
        score = jnp.where(causal[None], score, -jnp.inf)
        _, sel = lax.top_k(score, top_k)
        valid = sel <= qpos[None, :, None]
        k_sel = gather(k, sel)
        v_sel = gather(v, sel)
        sc = jnp.einsum('bqhd,bqkd->bqhk', qb, k_sel).astype(jnp.float32) * att_scale
        sc = jnp.where(valid[:, :, None, :], sc, -jnp.inf)
        p = jax.nn.softmax(sc, axis=-1).astype(v.dtype)
        return jnp.einsum('bqhk,bqkd->bqhd', p, v_sel)

    out = lax.map(block, jnp.arange(n_blk))
    return jnp.moveaxis(out, 0, 1).reshape(bsz, s, SPARSE_WIDTH)


def _diff_mixer(q, k, v, pos, lq1, lk1, lq2, lk2, subln_g, lambda_init):
    bsz, s, _ = q.shape
    q = _rope(q.reshape(bsz, s, 2 * DIFF_HEADS, DIFF_QK_DIM), pos).reshape(bsz, s, DIFF_HEADS, 2, DIFF_QK_DIM)
    k = _rope(k.reshape(bsz, s, 2 * DIFF_HEADS, DIFF_QK_DIM), pos).reshape(bsz, s, DIFF_HEADS, 2, DIFF_QK_DIM)
    v = v.reshape(bsz, s, DIFF_HEADS, DIFF_V_DIM)
    lam = (jnp.exp(jnp.sum(lq1.astype(jnp.float32) * lk1.astype(jnp.float32)))
           - jnp.exp(jnp.sum(lq2.astype(jnp.float32) * lk2.astype(jnp.float32))) + lambda_init)
    scale = DIFF_QK_DIM ** -0.5
    key_pos = jnp.arange(s)

    def block(i):
        start = i * Q_BLOCK
        qb = lax.dynamic_slice_in_dim(q, start, Q_BLOCK, axis=1)
        qpos = start + jnp.arange(Q_BLOCK)
        sc = jnp.einsum('bqhcd,bkhcd->bhcqk', qb, k).astype(jnp.float32) * scale
        causal = key_pos[None, :] <= qpos[:, None]
        sc = jnp.where(causal[None, None, None], sc, -jnp.inf)
        a = jax.nn.softmax(sc, axis=-1)
        p = (a[:, :, 0] - lam * a[:, :, 1]).astype(v.dtype)
        return jnp.einsum('bhqk,bkhd->bqhd', p, v)

    o = jnp.moveaxis(lax.map(block, jnp.arange(s // Q_BLOCK)), 0, 1).reshape(bsz, s, DIFF_HEADS, DIFF_V_DIM)
    o = _rmsnorm(o, subln_g, SUBLN_EPS) * (1.0 - lambda_init)
    return o.reshape(bsz, s, DIFF_WIDTH)


def setup_inputs(seed: int = 0) -> dict:
    key = jax.random.key(seed)
    ks = jax.random.split(key, 12)
    x = jax.random.normal(ks[0], (BATCH, SEQ, D_MODEL), jnp.float32)
    offset = jax.random.randint(ks[1], (BATCH, 1), 0, 4096, dtype=jnp.int32)
    positions = offset + jnp.arange(SEQ, dtype=jnp.int32)[None, :]
    norm_w = 1.0 + 0.02 * jax.random.normal(ks[2], (DEPTH, D_MODEL), jnp.float32)
    w_in = jax.random.normal(ks[3], (DEPTH, D_MODEL, N_IN), jnp.float32) * D_MODEL ** -0.5
    conv_w = jax.random.normal(ks[4], (DEPTH, CONV_K, CONV_WIDTH), jnp.float32) * CONV_K ** -0.5
    lam_q1 = 0.1 * jax.random.normal(ks[5], (DEPTH, DIFF_QK_DIM), jnp.float32)
    lam_k1 = 0.1 * jax.random.normal(ks[6], (DEPTH, DIFF_QK_DIM), jnp.float32)
    lam_q2 = 0.1 * jax.random.normal(ks[7], (DEPTH, DIFF_QK_DIM), jnp.float32)
    lam_k2 = 0.1 * jax.random.normal(ks[8], (DEPTH, DIFF_QK_DIM), jnp.float32)
    subln_w = 1.0 + 0.02 * jax.random.normal(ks[9], (DEPTH, DIFF_V_DIM), jnp.float32)
    w_out = jax.random.normal(ks[10], (DEPTH, MIX_WIDTH, D_MODEL), jnp.float32) * MIX_WIDTH ** -0.5
    final_norm_w = 1.0 + 0.02 * jax.random.normal(ks[11], (D_MODEL,), jnp.float32)
    return {"x": x, "positions": positions, "norm_w": norm_w, "w_in": w_in, "conv_w": conv_w,
            "lam_q1": lam_q1, "lam_k1": lam_k1, "lam_q2": lam_q2, "lam_k2": lam_k2,
            "subln_w": subln_w, "w_out": w_out, "final_norm_w": final_norm_w}


def reference(x, positions, norm_w, w_in, conv_w, lam_q1, lam_k1, lam_q2, lam_k2, subln_w, w_out, final_norm_w):
    for layer in range(DEPTH):
        lambda_init = 0.8 - 0.6 * math.exp(-0.3 * layer)
        h = _rmsnorm(x, norm_w[layer], NORM_EPS)
        u = jnp.einsum('bsd,dn->bsn', h, w_in[layer])
        (a_b, a_c, a_h, a_g,
         s_q, s_k, s_v, i_q, i_k, i_w, s_g,
         d_q, d_k, d_v, d_g) = _split_cols(u)
        y_conv = _short_conv_mixer(a_b, a_c, a_h, conv_w[layer]) * jax.nn.silu(a_g)
        y_sparse = _dsa_mixer(s_q, s_k, s_v, i_q, i_k, i_w, positions) * jax.nn.silu(s_g)
        y_diff = _diff_mixer(d_q, d_k, d_v, positions, lam_q1[layer], lam_k1[layer],
                             lam_q2[layer], lam_k2[layer], subln_w[layer], lambda_init) * jax.nn.silu(d_g)
        y = jnp.concatenate([y_conv, y_sparse, y_diff], axis=-1)
        x = x + jnp.einsum('bsm,md->bsd', y, w_out[layer])
    return _rmsnorm(x, final_norm_w, NORM_EPS)
```

```python
import functools
import math

import jax
import jax.numpy as jnp
import numpy as np
from jax import lax
from jax.experimental import pallas as pl
from jax.experimental.pallas import tpu as pltpu

D_MODEL = 2048
CONV_WIDTH = D_MODEL // 4
CONV_K = 3
SPARSE_HEADS = 6
SPARSE_HEAD_DIM = 128
SPARSE_WIDTH = SPARSE_HEADS * SPARSE_HEAD_DIM
IDX_HEADS = 8
IDX_DIM = 64
INDEX_TOPK_MAX = 256
DIFF_HEADS = 6
DIFF_QK_DIM = 64
DIFF_V_DIM = 2 * DIFF_QK_DIM
DIFF_WIDTH = DIFF_HEADS * DIFF_V_DIM
ROPE_THETA = 10000.0
NORM_EPS = 1e-6
SUBLN_EPS = 1e-5

LANE = 128
VMEM_LIMIT_BYTES = 56 * 1024 * 1024

U_SQ, U_SG, U_DQ, U_DK, U_DV, U_DG = 0, 6, 12, 18, 24, 30
U_AB, U_AC, U_AH, U_AG = 36, 40, 44, 48
U_IQ, U_SK, U_SV, U_IK, U_IW = 52, 56, 57, 58, 59
N_UNITS = 60
N_COLS = N_UNITS * LANE
UNITS_PER_TILE = 4
T_PLAIN, T_ROPE128, T_ROPE64 = 0, 1, 2
UNIT_TYPES = np.zeros((N_UNITS,), np.int32)
UNIT_TYPES[U_SQ:U_SQ + 6] = T_ROPE128
UNIT_TYPES[U_SK] = T_ROPE128
UNIT_TYPES[U_DQ:U_DQ + 6] = T_ROPE64
UNIT_TYPES[U_DK:U_DK + 6] = T_ROPE64
UNIT_TYPES[U_IQ:U_IQ + 4] = T_ROPE64
UNIT_TYPES[U_IK] = T_ROPE64

M_INIT = -1e30
MASK_BIAS = -2e30

BISECT_STEPS = 20


def _params(*sem):
    return pltpu.CompilerParams(dimension_semantics=sem, vmem_limit_bytes=VMEM_LIMIT_BYTES)


def _silu(x):
    return x / (1.0 + jnp.exp(-x))


def _dot_nt(a, b):
    return lax.dot_general(a, b, (((1,), (1,)), ((), ())), preferred_element_type=jnp.float32)


def _rope_tables_kernel(pos_ref, f128_ref, f64_ref, c128_ref, s128_ref, c64_ref, sa64_ref, sb64_ref):
    pos = pos_ref[...].astype(jnp.float32)
    lane = lax.broadcasted_iota(jnp.int32, c128_ref.shape, 1)
    ang = pos * f128_ref[...]
    sin = jnp.sin(ang)
    c128_ref[...] = jnp.cos(ang)
    s128_ref[...] = jnp.where(lane < SPARSE_HEAD_DIM // 2, -sin, sin)
    ang = pos * f64_ref[...]
    sin = jnp.sin(ang)
    first_half = (lane & (IDX_DIM - 1)) < IDX_DIM // 2
    c64_ref[...] = jnp.cos(ang)
    sa64_ref[...] = jnp.where(first_half, -sin, 0.0)
    sb64_ref[...] = jnp.where(first_half, 0.0, sin)


def _rope_tables(positions):
    m = positions.size
    ts = min(1024, m)
    pos = positions.reshape(m, 1)

    def inv_freq(d):
        half = d // 2
        return jnp.exp(-math.log(ROPE_THETA) * jnp.arange(half, dtype=jnp.float32) * (2.0 / d))

    f128 = jnp.tile(inv_freq(SPARSE_HEAD_DIM), LANE // (SPARSE_HEAD_DIM // 2)).reshape(1, LANE)
    f64 = jnp.tile(inv_freq(IDX_DIM), LANE // (IDX_DIM // 2)).reshape(1, LANE)
    tab = jax.ShapeDtypeStruct((m, LANE), jnp.float32)
    row = pl.BlockSpec((ts, LANE), lambda i: (i, 0))
    const = pl.BlockSpec((1, LANE), lambda i: (0, 0))
    return pl.pallas_call(
        _rope_tables_kernel,
        grid=(m // ts,),
        in_specs=[pl.BlockSpec((ts, 1), lambda i: (i, 0)), const, const],
        out_specs=[row] * 5,
        out_shape=[tab] * 5,
        compiler_params=_params("parallel"),
        name="rope_tables",
    )(pos, f128, f64)


def _inproj_kernel(types_ref, x_ref, g_ref, w_ref, c128_ref, s128_ref, c64_ref, sa64_ref, sb64_ref,
                   o_ref, h_scr, acc_scr, *, row_chunk):
    j = pl.program_id(1)
    tm = x_ref.shape[0]

    @pl.when(j == 0)
    def _():
        for r in range(0, tm, row_chunk):
            x = x_ref[r:r + row_chunk, :]
            ms = jnp.sum(x * x, axis=-1, keepdims=True) * (1.0 / D_MODEL)
            h_scr[r:r + row_chunk, :] = (x * lax.rsqrt(ms + NORM_EPS) * g_ref[...]).astype(jnp.bfloat16)

    acc_scr[...] = jnp.dot(h_scr[...], w_ref[...], preferred_element_type=jnp.float32)

    for u in range(UNITS_PER_TILE):
        cols = slice(u * LANE, (u + 1) * LANE)
        unit_type = types_ref[j * UNITS_PER_TILE + u]

        @pl.when(unit_type == T_PLAIN)
        def _():
            o_ref[:, cols] = acc_scr[:, cols].astype(o_ref.dtype)

        @pl.when(unit_type == T_ROPE128)
        def _():
            a = acc_scr[:, cols]
            r = a * c128_ref[...] + pltpu.roll(a, SPARSE_HEAD_DIM // 2, 1) * s128_ref[...]
            o_ref[:, cols] = r.astype(o_ref.dtype)

        @pl.when(unit_type == T_ROPE64)
        def _():
            a = acc_scr[:, cols]
            r = (a * c64_ref[...] + pltpu.roll(a, LANE - IDX_DIM // 2, 1) * sa64_ref[...]
                 + pltpu.roll(a, IDX_DIM // 2, 1) * sb64_ref[...])
            o_ref[:, cols] = r.astype(o_ref.dtype)


def _inproj(x2d, norm_g, w_perm, tables):
    m = x2d.shape[0]
    tm = min(1024, m)
    tn = UNITS_PER_TILE * LANE
    row_tab = pl.BlockSpec((tm, LANE), lambda i, j, t: (i, 0))
    grid_spec = pltpu.PrefetchScalarGridSpec(
        num_scalar_prefetch=1,
        grid=(m // tm, N_COLS // tn),
        in_specs=[
            pl.BlockSpec((tm, D_MODEL), lambda i, j, t: (i, 0)),
            pl.BlockSpec((1, D_MODEL), lambda i, j, t: (0, 0)),
            pl.BlockSpec((D_MODEL, tn), lambda i, j, t: (0, j)),
            row_tab, row_tab, row_tab, row_tab, row_tab,
        ],
        out_specs=pl.BlockSpec((tm, tn), lambda i, j, t: (i, j)),
        scratch_shapes=[pltpu.VMEM((tm, D_MODEL), jnp.bfloat16), pltpu.VMEM((tm, tn), jnp.float32)],
    )
    return pl.pallas_call(
        functools.partial(_inproj_kernel, row_chunk=min(256, tm)),
        grid_spec=grid_spec,
        out_shape=jax.ShapeDtypeStruct((m, N_COLS), jnp.bfloat16),
        compiler_params=_params("parallel", "arbitrary"),
        name="inproj",
    )(jnp.asarray(UNIT_TYPES), x2d, norm_g.reshape(1, D_MODEL), w_perm, *tables)


def _permute_w_in(w):
    offs = np.cumsum([0, CONV_WIDTH, CONV_WIDTH, CONV_WIDTH, CONV_WIDTH, SPARSE_WIDTH, SPARSE_HEAD_DIM,
                      SPARSE_HEAD_DIM, IDX_HEADS * IDX_DIM, IDX_DIM, IDX_HEADS, SPARSE_WIDTH,
                      2 * DIFF_HEADS * DIFF_QK_DIM, 2 * DIFF_HEADS * DIFF_QK_DIM, DIFF_WIDTH, DIFF_WIDTH])
    (a_b, a_c, a_h, a_g, s_q, s_k, s_v, i_q, i_k, i_w, s_g, d_q, d_k, d_v, d_g) = [
        w[:, offs[n]:offs[n + 1]] for n in range(15)]
    pad = jnp.zeros((w.shape[0], LANE - IDX_HEADS), w.dtype)
    wp = jnp.concatenate([s_q, s_g, d_q, d_k, d_v, d_g, a_b, a_c, a_h, a_g, i_q, s_k, s_v, i_k, i_k, i_w, pad],
                         axis=1)
    assert wp.shape[1] == N_COLS
    return wp.astype(jnp.bfloat16)


def _conv_kernel(b_ref, c_ref, h_ref, g_ref, w_ref, o_ref):
    z = c_ref[...].astype(jnp.float32) * h_ref[...].astype(jnp.float32)
    row = lax.broadcasted_iota(jnp.int32, z.shape, 0)
    w = w_ref[...]
    conv = w[CONV_K - 1:CONV_K, :] * z
    for back in range(1, CONV_K):
        shifted = jnp.where(row >= back, pltpu.roll(z, back, 0), 0.0)
        conv = conv + w[CONV_K - 1 - back:CONV_K - back, :] * shifted
    y = b_ref[...].astype(jnp.float32) * conv * _silu(g_ref[...].astype(jnp.float32))
    o_ref[...] = y.astype(o_ref.dtype)


def _conv_mixer(u, conv_w, bsz, seq):
    n_units = CONV_WIDTH // LANE

    def spec(unit0):
        return pl.BlockSpec((seq, LANE), lambda b, c: (b, unit0 + c))

    return pl.pallas_call(
        _conv_kernel,
        grid=(bsz, n_units),
        in_specs=[spec(U_AB), spec(U_AC), spec(U_AH), spec(U_AG),
                  pl.BlockSpec((CONV_K, LANE), lambda b, c: (0, c))],
        out_specs=pl.BlockSpec((seq, LANE), lambda b, c: (b, c)),
        out_shape=jax.ShapeDtypeStruct((bsz * seq, CONV_WIDTH), jnp.bfloat16),
        compiler_params=_params("parallel", "parallel"),
        name="conv_mixer",
    )(u, u, u, u, conv_w)


def _dsa_kernel(q_ref, g_ref, iq_ref, iw_ref, ki_ref, k_ref, v_ref, o_ref, sc_scr, *, top_k):
    tq = q_ref.shape[0]
    kc = sc_scr.shape[2]
    i = pl.program_id(1)
    n_chunks = (i * tq + tq + kc - 1) // kc
    idx_scale = (IDX_DIM * IDX_HEADS) ** -0.5
    att_scale = SPARSE_HEAD_DIM ** -0.5
    f32 = jnp.float32
    top_kf = float(top_k)

    row = i * tq + lax.broadcasted_iota(jnp.int32, (tq, kc), 0)
    col_local = lax.broadcasted_iota(jnp.int32, (tq, kc), 1)
    lane = lax.broadcasted_iota(jnp.int32, (tq, LANE), 1)
    low_half = lane < IDX_DIM

    q_heads = []
    for unit in range(IDX_HEADS * IDX_DIM // LANE):
        qu = iq_ref[:, unit * LANE:(unit + 1) * LANE].astype(f32)
        q_heads.append(jnp.where(low_half, qu, 0.0).astype(jnp.bfloat16))
        q_heads.append(jnp.where(low_half, 0.0, qu).astype(jnp.bfloat16))
    w = iw_ref[...].astype(f32) * idx_scale
    w_heads = [jnp.broadcast_to(w[:, h:h + 1], (tq, kc)) for h in range(IDX_HEADS)]

    def index_chunk(c, carry):
        rmax, rmin = carry
        start = pl.multiple_of(c * kc, kc)
        kic = ki_ref[pl.ds(start, kc), :]
        score = jnp.zeros((tq, kc), f32)
        for h in range(IDX_HEADS):
            score = score + w_heads[h] * jnp.maximum(_dot_nt(q_heads[h], kic), 0.0)
        causal = (start + col_local) <= row
        sc_scr[c] = jnp.where(causal, score, -jnp.inf)
        rmax = jnp.maximum(rmax, jnp.max(jnp.where(causal, score, -jnp.inf), axis=1, keepdims=True))
        rmin = jnp.minimum(rmin, jnp.min(jnp.where(causal, score, jnp.inf), axis=1, keepdims=True))
        return rmax, rmin

    rmax, rmin = lax.fori_loop(0, n_chunks, index_chunk,
                               (jnp.full((tq, 1), -jnp.inf, f32), jnp.full((tq, 1), jnp.inf, f32)))

    def lane_fold(x):
        acc = x[:, :LANE]
        for t in range(1, kc // LANE):
            acc = acc + x[:, t * LANE:(t + 1) * LANE]
        return acc

    def count(pred):
        def body(c, acc):
            return acc + lane_fold(jnp.where(pred(sc_scr[c]), 1.0, 0.0))
        acc = lax.fori_loop(0, n_chunks, body, jnp.zeros((tq, LANE), f32))
        return jnp.sum(acc, axis=1, keepdims=True)

    def row_min_where(pred):
        def body(c, acc):
            s = sc_scr[c]
            x = jnp.where(pred(s), s, jnp.inf)
            m = x[:, :LANE]
            for t in range(1, kc // LANE):
                m = jnp.minimum(m, x[:, t * LANE:(t + 1) * LANE])
            return jnp.minimum(acc, m)
        acc = lax.fori_loop(0, n_chunks, body, jnp.full((tq, LANE), jnp.inf, f32))
        return jnp.min(acc, axis=1, keepdims=True)

    def bcast(p):
        return jnp.broadcast_to(p, (tq, kc))

    def bisect(_, carry):
        lo, hi = carry
        p = 0.5 * (lo + hi)
        pb = bcast(p)
        enough = count(lambda s: s >= pb) >= top_kf
        return jnp.where(enough, p, lo), jnp.where(enough, hi, p)

    lo, _ = lax.fori_loop(0, BISECT_STEPS, bisect, (rmin, rmax))
    lob = bcast(lo)
    thr = row_min_where(lambda s: s >= lob)
    thrb = bcast(thr)
    n_gt = count(lambda s: s > thrb)

    def unsettled(carry):
        _, n_gt = carry
        return jnp.max(n_gt) >= top_kf

    def walk(carry):
        thr, n_gt = carry
        thrb = bcast(thr)
        nxt = row_min_where(lambda s: s > thrb)
        nxtb = bcast(nxt)
        n_nxt = count(lambda s: s > nxtb)
        move = n_gt >= top_kf
        return jnp.where(move, nxt, thr), jnp.where(move, n_nxt, n_gt)

    thr, n_gt = lax.while_loop(unsettled, walk, (thr, n_gt))
    thrb = bcast(thr)
    need = top_kf - n_gt

    q = q_ref[...].astype(f32) * att_scale
    q6 = jnp.concatenate([q[:, h * LANE:(h + 1) * LANE] for h in range(SPARSE_HEADS)], axis=0).astype(jnp.bfloat16)
    before = (lax.broadcasted_iota(jnp.int32, (kc, kc), 0) < lax.broadcasted_iota(jnp.int32, (kc, kc), 1))
    before = jnp.where(before, 1.0, 0.0).astype(jnp.bfloat16)
    rows6 = SPARSE_HEADS * tq

    def attend_chunk(c, carry):
        m_old, l_old, acc, n_eq_before = carry
        start = pl.multiple_of(c * kc, kc)
        s_idx = sc_scr[c]
        is_eq = s_idx == thrb
        eq_f = jnp.where(is_eq, 1.0, 0.0)
        rank = n_eq_before + jnp.dot(eq_f.astype(jnp.bfloat16), before, preferred_element_type=f32)
        selected = (s_idx > thrb) | (is_eq & (rank < need))
        bias = jnp.where(selected, 0.0, MASK_BIAS)
        bias6 = jnp.concatenate([bias] * SPARSE_HEADS, axis=0)
        s = _dot_nt(q6, k_ref[pl.ds(start, kc), :]) + bias6
        m_new = jnp.maximum(m_old, jnp.max(s, axis=1, keepdims=True))
        p = jnp.exp(s - m_new)
        alpha = jnp.exp(m_old - m_new)
        l_new = alpha * l_old + jnp.sum(p, axis=1, keepdims=True)
        acc = alpha * acc + jnp.dot(p.astype(jnp.bfloat16), v_ref[pl.ds(start, kc), :],
                                    preferred_element_type=f32)
        return m_new, l_new, acc, n_eq_before + jnp.sum(eq_f, axis=1, keepdims=True)

    init = (jnp.full((rows6, 1), M_INIT, f32), jnp.zeros((rows6, 1), f32),
            jnp.zeros((rows6, SPARSE_HEAD_DIM), f32), jnp.zeros((tq, 1), f32))
    _, l_fin, acc, _ = lax.fori_loop(0, n_chunks, attend_chunk, init)
    out = acc / l_fin
    gate = _silu(g_ref[...].astype(f32))
    for h in range(SPARSE_HEADS):
        cols = slice(h * LANE, (h + 1) * LANE)
        o_ref[:, cols] = (out[h * tq:(h + 1) * tq, :] * gate[:, cols]).astype(o_ref.dtype)


def _dsa_mixer(u, bsz, seq):
    tq = min(256, seq)
    kc = tq
    nq = seq // tq
    top_k = min(INDEX_TOPK_MAX, seq // 4)
    wide = SPARSE_WIDTH // LANE
    iq_wide = IDX_HEADS * IDX_DIM // LANE

    def q_spec(width_units, unit0):
        assert unit0 % width_units == 0
        return pl.BlockSpec((tq, width_units * LANE), lambda b, i: (b * nq + i, unit0 // width_units))

    def kv_spec(unit):
        return pl.BlockSpec((seq, LANE), lambda b, i: (b, unit))

    return pl.pallas_call(
        functools.partial(_dsa_kernel, top_k=top_k),
        grid=(bsz, nq),
        in_specs=[q_spec(wide, U_SQ), q_spec(wide, U_SG), q_spec(iq_wide, U_IQ), q_spec(1, U_IW),
                  kv_spec(U_IK), kv_spec(U_SK), kv_spec(U_SV)],
        out_specs=pl.BlockSpec((tq, SPARSE_WIDTH), lambda b, i: (b * nq + i, 0)),
        out_shape=jax.ShapeDtypeStruct((bsz * seq, SPARSE_WIDTH), jnp.bfloat16),
        scratch_shapes=[pltpu.VMEM((seq // kc, tq, kc), jnp.float32)],
        compiler_params=_params("parallel", "arbitrary"),
        name="dsa_mixer",
    )(u, u, u, u, u, u, u)


def _diff_kernel(q_ref, g_ref, k_ref, v_ref, lq1_ref, lk1_ref, lq2_ref, lk2_ref, sg_ref, o_ref, *, lambda_init):
    tq = q_ref.shape[0]
    kc = tq
    i = pl.program_id(2)
    f32 = jnp.float32
    scale = DIFF_QK_DIM ** -0.5

    lam = (jnp.exp(jnp.sum(lq1_ref[...] * lk1_ref[...], axis=1, keepdims=True))
           - jnp.exp(jnp.sum(lq2_ref[...] * lk2_ref[...], axis=1, keepdims=True)) + lambda_init)

    lane = lax.broadcasted_iota(jnp.int32, (tq, LANE), 1)
    first = lane < DIFF_QK_DIM
    q = q_ref[...].astype(f32) * scale
    q2 = jnp.concatenate([jnp.where(first, q, 0.0), jnp.where(first, 0.0, q)], axis=0).astype(jnp.bfloat16)
    above_diag = (lax.broadcasted_iota(jnp.int32, (tq, kc), 1) > lax.broadcasted_iota(jnp.int32, (tq, kc), 0))
    diag_bias = jnp.where(above_diag, MASK_BIAS, 0.0)
    diag_bias2 = jnp.concatenate([diag_bias, diag_bias], axis=0)

    def chunk(c, carry, on_diagonal):
        m_old, l_old, acc = carry
        start = pl.multiple_of(c * kc, kc)
        s = _dot_nt(q2, k_ref[pl.ds(start, kc), :])
        if on_diagonal:
            s = s + diag_bias2
        m_new = jnp.maximum(m_old, jnp.max(s, axis=1, keepdims=True))
        p = jnp.exp(s - m_new)
        alpha = jnp.exp(m_old - m_new)
        l_new = alpha * l_old + jnp.sum(p, axis=1, keepdims=True)
        acc = alpha * acc + jnp.dot(p.astype(jnp.bfloat16), v_ref[pl.ds(start, kc), :],
                                    preferred_element_type=f32)
        return m_new, l_new, acc

    init = (jnp.full((2 * tq, 1), M_INIT, f32), jnp.zeros((2 * tq, 1), f32), jnp.zeros((2 * tq, DIFF_V_DIM), f32))
    carry = lax.fori_loop(0, i, lambda c, cr: chunk(c, cr, False), init)
    _, l_fin, acc = chunk(i, carry, True)
    o = acc / l_fin
    o = o[:tq, :] - lam * o[tq:, :]
    ms = jnp.sum(o * o, axis=1, keepdims=True) * (1.0 / DIFF_V_DIM)
    o = o * lax.rsqrt(ms + SUBLN_EPS) * sg_ref[...] * (1.0 - lambda_init)
    o_ref[...] = (o * _silu(g_ref[...].astype(f32))).astype(o_ref.dtype)


def _diff_mixer(u, lq1, lk1, lq2, lk2, subln_g, lambda_init, bsz, seq):
    tq = min(256, seq)
    nq = seq // tq

    def q_spec(unit0):
        return pl.BlockSpec((tq, LANE), lambda b, h, i: (b * nq + i, unit0 + h))

    def kv_spec(unit0):
        return pl.BlockSpec((seq, LANE), lambda b, h, i: (b, unit0 + h))

    def vec_spec(n):
        return pl.BlockSpec((1, n), lambda b, h, i: (0, 0))

    return pl.pallas_call(
        functools.partial(_diff_kernel, lambda_init=lambda_init),
        grid=(bsz, DIFF_HEADS, nq),
        in_specs=[q_spec(U_DQ), q_spec(U_DG), kv_spec(U_DK), kv_spec(U_DV),
                  vec_spec(DIFF_QK_DIM), vec_spec(DIFF_QK_DIM), vec_spec(DIFF_QK_DIM), vec_spec(DIFF_QK_DIM),
                  vec_spec(DIFF_V_DIM)],
        out_specs=pl.BlockSpec((tq, LANE), lambda b, h, i: (b * nq + i, h)),
        out_shape=jax.ShapeDtypeStruct((bsz * seq, DIFF_WIDTH), jnp.bfloat16),
        compiler_params=_params("parallel", "parallel", "arbitrary"),
        name="diff_mixer",
    )(u, u, u, u, lq1.reshape(1, -1), lk1.reshape(1, -1), lq2.reshape(1, -1), lk2.reshape(1, -1),
      subln_g.reshape(1, -1))


def _outproj_kernel(x_ref, yc_ref, ys_ref, yd_ref, w_ref, g_ref, o_ref, *, final_norm):
    acc = x_ref[...]
    acc = acc + jnp.dot(yc_ref[...], w_ref[0:CONV_WIDTH, :], preferred_element_type=jnp.float32)
    acc = acc + jnp.dot(ys_ref[...], w_ref[CONV_WIDTH:CONV_WIDTH + SPARSE_WIDTH, :],
                        preferred_element_type=jnp.float32)
    acc = acc + jnp.dot(yd_ref[...], w_ref[CONV_WIDTH + SPARSE_WIDTH:, :], preferred_element_type=jnp.float32)
    if final_norm:
        ms = jnp.sum(acc * acc, axis=-1, keepdims=True) * (1.0 / D_MODEL)
        acc = acc * lax.rsqrt(ms + NORM_EPS) * g_ref[...]
    o_ref[...] = acc


def _outproj(x2d, y_conv, y_sparse, y_diff, w_out_bf16, final_g, final_norm):
    m = x2d.shape[0]
    tm = min(512, m)

    def rows(width):
        return pl.BlockSpec((tm, width), lambda i: (i, 0))

    return pl.pallas_call(
        functools.partial(_outproj_kernel, final_norm=final_norm),
        grid=(m // tm,),
        in_specs=[rows(D_MODEL), rows(CONV_WIDTH), rows(SPARSE_WIDTH), rows(DIFF_WIDTH),
                  pl.BlockSpec((D_MODEL, D_MODEL), lambda i: (0, 0)),
                  pl.BlockSpec((1, D_MODEL), lambda i: (0, 0))],
        out_specs=rows(D_MODEL),
        out_shape=jax.ShapeDtypeStruct((m, D_MODEL), jnp.float32),
        compiler_params=_params("parallel"),
        name="outproj",
    )(x2d, y_conv, y_sparse, y_diff, w_out_bf16, final_g.reshape(1, D_MODEL))


def kernel(x, positions, norm_w, w_in, conv_w, lam_q1, lam_k1, lam_q2, lam_k2, subln_w, w_out, final_norm_w):
    bsz, seq, _ = x.shape
    depth = norm_w.shape[0]
    x2d = x.reshape(bsz * seq, D_MODEL)
    tables = _rope_tables(positions)
    for layer in range(depth):
        lambda_init = 0.8 - 0.6 * math.exp(-0.3 * layer)
        u = _inproj(x2d, norm_w[layer], _permute_w_in(w_in[layer]), tables)
        y_conv = _conv_mixer(u, conv_w[layer], bsz, seq)
        y_sparse = _dsa_mixer(u, bsz, seq)
        y_diff = _diff_mixer(u, lam_q1[layer], lam_k1[layer], lam_q2[layer], lam_k2[layer], subln_w[layer],
                             lambda_init, bsz, seq)
        x2d = _outproj(x2d, y_conv, y_sparse, y_diff, w_out[layer].astype(jnp.bfloat16), final_norm_w,
                       final_norm=(layer == depth - 1))
    return x2d.reshape(bsz, seq, D_MODEL)
```

```python
import functools
import math

import jax
import jax.numpy as jnp
import numpy as np
from jax import lax
from jax.experimental import pallas as pl
from jax.experimental.pallas import tpu as pltpu

D_MODEL = 2048
CONV_WIDTH = D_MODEL // 4
CONV_K = 3
SPARSE_HEADS = 6
SPARSE_HEAD_DIM = 128
SPARSE_WIDTH = SPARSE_HEADS * SPARSE_HEAD_DIM
IDX_HEADS = 8
IDX_DIM = 64
INDEX_TOPK_MAX = 256
DIFF_HEADS = 6
DIFF_QK_DIM = 64
DIFF_V_DIM = 2 * DIFF_QK_DIM
DIFF_WIDTH = DIFF_HEADS * DIFF_V_DIM
ROPE_THETA = 10000.0
NORM_EPS = 1e-6
SUBLN_EPS = 1e-5

LANE = 128
SUBLANE = 8
VMEM_LIMIT_BYTES = 56 * 1024 * 1024

U_SQ, U_SG, U_DQ, U_DK, U_DV, U_DG = 0, 6, 12, 18, 24, 30
U_AB, U_AC, U_AH, U_AG = 36, 40, 44, 48
U_IQ, U_SK, U_SV, U_IK, U_IW = 52, 56, 57, 58, 59
N_UNITS = 60
N_COLS = N_UNITS * LANE
UNITS_PER_TILE = 4
T_PLAIN, T_ROPE128, T_ROPE64 = 0, 1, 2
UNIT_TYPES = np.zeros((N_UNITS,), np.int32)
UNIT_TYPES[U_SQ:U_SQ + 6] = T_ROPE128
UNIT_TYPES[U_SK] = T_ROPE128
UNIT_TYPES[U_DQ:U_DQ + 6] = T_ROPE64
UNIT_TYPES[U_DK:U_DK + 6] = T_ROPE64
UNIT_TYPES[U_IQ:U_IQ + 4] = T_ROPE64
UNIT_TYPES[U_IK] = T_ROPE64

M_INIT = -1e30
MASK_BIAS = -2e30
LOG2E = math.log2(math.e)

BISECT_STEPS = 20


def _params(*sem):
    return pltpu.CompilerParams(dimension_semantics=sem, vmem_limit_bytes=VMEM_LIMIT_BYTES)


def _silu(x):
    return x / (1.0 + jnp.exp(-x))


def _dot_nt(a, b):
    return lax.dot_general(a, b, (((1,), (1,)), ((), ())), preferred_element_type=jnp.float32)


def _tree(op, parts):
    parts = list(parts)
    while len(parts) > 1:
        nxt = [op(parts[n], parts[n + 1]) for n in range(0, len(parts) - 1, 2)]
        if len(parts) % 2:
            nxt.append(parts[-1])
        parts = nxt
    return parts[0]


def _fold_lanes(op, x):
    return _tree(op, [x[:, t * LANE:(t + 1) * LANE] for t in range(x.shape[1] // LANE)])


def _fold_rows(op, x):
    return _tree(op, [x[g * SUBLANE:(g + 1) * SUBLANE, :] for g in range(x.shape[0] // SUBLANE)])


def _rope_tables_kernel(pos_ref, f128_ref, f64_ref, c128_ref, s128_ref, c64_ref, sa64_ref, sb64_ref):
    pos = pos_ref[...].astype(jnp.float32)
    lane = lax.broadcasted_iota(jnp.int32, c128_ref.shape, 1)
    ang = pos * f128_ref[...]
    sin = jnp.sin(ang)
    c128_ref[...] = jnp.cos(ang)
    s128_ref[...] = jnp.where(lane < SPARSE_HEAD_DIM // 2, -sin, sin)
    ang = pos * f64_ref[...]
    sin = jnp.sin(ang)
    first_half = (lane & (IDX_DIM - 1)) < IDX_DIM // 2
    c64_ref[...] = jnp.cos(ang)
    sa64_ref[...] = jnp.where(first_half, -sin, 0.0)
    sb64_ref[...] = jnp.where(first_half, 0.0, sin)


def _rope_tables(positions):
    m = positions.size
    ts = min(1024, m)
    pos = positions.reshape(m, 1)

    def inv_freq(d):
        half = d // 2
        return jnp.exp(-math.log(ROPE_THETA) * jnp.arange(half, dtype=jnp.float32) * (2.0 / d))

    f128 = jnp.tile(inv_freq(SPARSE_HEAD_DIM), LANE // (SPARSE_HEAD_DIM // 2)).reshape(1, LANE)
    f64 = jnp.tile(inv_freq(IDX_DIM), LANE // (IDX_DIM // 2)).reshape(1, LANE)
    tab = jax.ShapeDtypeStruct((m, LANE), jnp.float32)
    row = pl.BlockSpec((ts, LANE), lambda i: (i, 0))
    const = pl.BlockSpec((1, LANE), lambda i: (0, 0))
    return pl.pallas_call(
        _rope_tables_kernel,
        grid=(m // ts,),
        in_specs=[pl.BlockSpec((ts, 1), lambda i: (i, 0)), const, const],
        out_specs=[row] * 5,
        out_shape=[tab] * 5,
        compiler_params=_params("parallel"),
        name="rope_tables",
    )(pos, f128, f64)


def _inproj_kernel(types_ref, x_ref, g_ref, w_ref, c128_ref, s128_ref, c64_ref, sa64_ref, sb64_ref,
                   o_ref, h_scr, acc_scr, *, row_chunk):
    j = pl.program_id(1)
    tm = x_ref.shape[0]

    @pl.when(j == 0)
    def _():
        for r in range(0, tm, row_chunk):
            x = x_ref[r:r + row_chunk, :]
            ms = jnp.sum(x * x, axis=-1, keepdims=True) * (1.0 / D_MODEL)
            h_scr[r:r + row_chunk, :] = (x * lax.rsqrt(ms + NORM_EPS) * g_ref[...]).astype(jnp.bfloat16)

    acc_scr[...] = jnp.dot(h_scr[...], w_ref[...], preferred_element_type=jnp.float32)

    for u in range(UNITS_PER_TILE):
        cols = slice(u * LANE, (u + 1) * LANE)
        unit_type = types_ref[j * UNITS_PER_TILE + u]

        @pl.when(unit_type == T_PLAIN)
        def _():
            o_ref[:, cols] = acc_scr[:, cols].astype(o_ref.dtype)

        @pl.when(unit_type == T_ROPE128)
        def _():
            a = acc_scr[:, cols]
            r = a * c128_ref[...] + pltpu.roll(a, SPARSE_HEAD_DIM // 2, 1) * s128_ref[...]
            o_ref[:, cols] = r.astype(o_ref.dtype)

        @pl.when(unit_type == T_ROPE64)
        def _():
            a = acc_scr[:, cols]
            r = (a * c64_ref[...] + pltpu.roll(a, LANE - IDX_DIM // 2, 1) * sa64_ref[...]
                 + pltpu.roll(a, IDX_DIM // 2, 1) * sb64_ref[...])
            o_ref[:, cols] = r.astype(o_ref.dtype)


def _inproj(x2d, norm_g, w_perm, tables):
    m = x2d.shape[0]
    tm = min(1024, m)
    tn = UNITS_PER_TILE * LANE
    row_tab = pl.BlockSpec((tm, LANE), lambda i, j, t: (i, 0))
    grid_spec = pltpu.PrefetchScalarGridSpec(
        num_scalar_prefetch=1,
        grid=(m // tm, N_COLS // tn),
        in_specs=[
            pl.BlockSpec((tm, D_MODEL), lambda i, j, t: (i, 0)),
            pl.BlockSpec((1, D_MODEL), lambda i, j, t: (0, 0)),
            pl.BlockSpec((D_MODEL, tn), lambda i, j, t: (0, j)),
            row_tab, row_tab, row_tab, row_tab, row_tab,
        ],
        out_specs=pl.BlockSpec((tm, tn), lambda i, j, t: (i, j)),
        scratch_shapes=[pltpu.VMEM((tm, D_MODEL), jnp.bfloat16), pltpu.VMEM((tm, tn), jnp.float32)],
    )
    return pl.pallas_call(
        functools.partial(_inproj_kernel, row_chunk=min(256, tm)),
        grid_spec=grid_spec,
        out_shape=jax.ShapeDtypeStruct((m, N_COLS), jnp.bfloat16),
        compiler_params=_params("parallel", "arbitrary"),
        name="inproj",
    )(jnp.asarray(UNIT_TYPES), x2d, norm_g.reshape(1, D_MODEL), w_perm, *tables)


def _permute_w_in(w):
    offs = np.cumsum([0, CONV_WIDTH, CONV_WIDTH, CONV_WIDTH, CONV_WIDTH, SPARSE_WIDTH, SPARSE_HEAD_DIM,
                      SPARSE_HEAD_DIM, IDX_HEADS * IDX_DIM, IDX_DIM, IDX_HEADS, SPARSE_WIDTH,
                      2 * DIFF_HEADS * DIFF_QK_DIM, 2 * DIFF_HEADS * DIFF_QK_DIM, DIFF_WIDTH, DIFF_WIDTH])
    (a_b, a_c, a_h, a_g, s_q, s_k, s_v, i_q, i_k, i_w, s_g, d_q, d_k, d_v, d_g) = [
        w[:, offs[n]:offs[n + 1]] for n in range(15)]
    pad = jnp.zeros((w.shape[0], LANE - IDX_HEADS), w.dtype)
    wp = jnp.concatenate([s_q, s_g, d_q, d_k, d_v, d_g, a_b, a_c, a_h, a_g, i_q, s_k, s_v, i_k, i_k, i_w, pad],
                         axis=1)
    assert wp.shape[1] == N_COLS
    return wp.astype(jnp.bfloat16)


def _conv_kernel(b_ref, c_ref, h_ref, g_ref, w_ref, o_ref):
    z = c_ref[...].astype(jnp.float32) * h_ref[...].astype(jnp.float32)
    row = lax.broadcasted_iota(jnp.int32, z.shape, 0)
    w = w_ref[...]
    conv = w[CONV_K - 1:CONV_K, :] * z
    for back in range(1, CONV_K):
        shifted = jnp.where(row >= back, pltpu.roll(z, back, 0), 0.0)
        conv = conv + w[CONV_K - 1 - back:CONV_K - back, :] * shifted
    y = b_ref[...].astype(jnp.float32) * conv * _silu(g_ref[...].astype(jnp.float32))
    o_ref[...] = y.astype(o_ref.dtype)


def _conv_mixer(u, conv_w, bsz, seq):
    n_units = CONV_WIDTH // LANE

    def spec(unit0):
        return pl.BlockSpec((seq, LANE), lambda b, c: (b, unit0 + c))

    return pl.pallas_call(
        _conv_kernel,
        grid=(bsz, n_units),
        in_specs=[spec(U_AB), spec(U_AC), spec(U_AH), spec(U_AG),
                  pl.BlockSpec((CONV_K, LANE), lambda b, c: (0, c))],
        out_specs=pl.BlockSpec((seq, LANE), lambda b, c: (b, c)),
        out_shape=jax.ShapeDtypeStruct((bsz * seq, CONV_WIDTH), jnp.bfloat16),
        compiler_params=_params("parallel", "parallel"),
        name="conv_mixer",
    )(u, u, u, u, conv_w)


def _dsa_kernel(q_ref, g_ref, iq_ref, iw_ref, ki_ref, k_ref, v_ref, o_ref, sc_scr, m_scr, l_scr, acc_scr, *,
                top_k):
    tq = q_ref.shape[0]
    kc = tq
    i = pl.program_id(1)
    n_chunks = i + 1
    idx_scale = (IDX_DIM * IDX_HEADS) ** -0.5
    att_scale = SPARSE_HEAD_DIM ** -0.5
    f32 = jnp.float32
    bf16 = jnp.bfloat16
    top_kf = float(top_k)

    def key_rows(c):
        return pl.ds(pl.multiple_of(c * kc, kc), kc)

    lane = lax.broadcasted_iota(jnp.int32, (tq, LANE), 1)
    low_half = lane < IDX_DIM
    q_heads = []
    for unit in range(IDX_HEADS * IDX_DIM // LANE):
        qu = iq_ref[:, unit * LANE:(unit + 1) * LANE].astype(f32)
        q_heads.append(jnp.where(low_half, qu, 0.0).astype(bf16))
        q_heads.append(jnp.where(low_half, 0.0, qu).astype(bf16))
    w_t = jnp.transpose(iw_ref[...].astype(f32)) * idx_scale
    w_rows = [w_t[h:h + 1, :] for h in range(IDX_HEADS)]
    above_diag = (lax.broadcasted_iota(jnp.int32, (kc, tq), 0) > lax.broadcasted_iota(jnp.int32, (kc, tq), 1))

    def index_chunk(c, carry, on_diagonal):
        rmax8, rmin8 = carry
        kic = ki_ref[key_rows(c), :]
        score = w_rows[0] * jnp.maximum(_dot_nt(kic, q_heads[0]), 0.0)
        for h in range(1, IDX_HEADS):
            score = score + w_rows[h] * jnp.maximum(_dot_nt(kic, q_heads[h]), 0.0)
        if on_diagonal:
            hi_part = jnp.where(above_diag, -jnp.inf, score)
            lo_part = jnp.where(above_diag, jnp.inf, score)
        else:
            hi_part = lo_part = score
        sc_scr[c] = hi_part
        return (jnp.maximum(rmax8, _fold_rows(jnp.maximum, hi_part)),
                jnp.minimum(rmin8, _fold_rows(jnp.minimum, lo_part)))

    carry = (jnp.full((SUBLANE, tq), -jnp.inf, f32), jnp.full((SUBLANE, tq), jnp.inf, f32))
    carry = lax.fori_loop(0, i, lambda c, cr: index_chunk(c, cr, False), carry)
    rmax8, rmin8 = index_chunk(i, carry, True)
    rmax = jnp.max(rmax8, axis=0, keepdims=True)
    rmin = jnp.min(rmin8, axis=0, keepdims=True)

    def count(pred):
        def body(c, acc8):
            return acc8 + _fold_rows(jnp.add, jnp.where(pred(sc_scr[c]), 1.0, 0.0))
        acc8 = lax.fori_loop(0, n_chunks, body, jnp.zeros((SUBLANE, tq), f32))
        return jnp.sum(acc8, axis=0, keepdims=True)

    def min_where(pred):
        def body(c, acc8):
            s = sc_scr[c]
            return jnp.minimum(acc8, _fold_rows(jnp.minimum, jnp.where(pred(s), s, jnp.inf)))
        acc8 = lax.fori_loop(0, n_chunks, body, jnp.full((SUBLANE, tq), jnp.inf, f32))
        return jnp.min(acc8, axis=0, keepdims=True)

    def bisect(_, carry):
        lo, hi = carry
        p = 0.5 * (lo + hi)
        enough = count(lambda s: s >= p) >= top_kf
        return jnp.where(enough, p, lo), jnp.where(enough, hi, p)

    lo, _ = lax.fori_loop(0, BISECT_STEPS, bisect, (rmin, rmax))
    thr = min_where(lambda s: s >= lo)
    n_gt = count(lambda s: s > thr)

    def unsettled(carry):
        return jnp.max(carry[1]) >= top_kf

    def walk(carry):
        thr, n_gt = carry
        nxt = min_where(lambda s: s > thr)
        n_nxt = count(lambda s: s > nxt)
        move = n_gt >= top_kf
        return jnp.where(move, nxt, thr), jnp.where(move, n_nxt, n_gt)

    thr, n_gt = lax.while_loop(unsettled, walk, (thr, n_gt))
    need = top_kf - n_gt

    earlier = (lax.broadcasted_iota(jnp.int32, (kc, kc), 1) < lax.broadcasted_iota(jnp.int32, (kc, kc), 0))
    earlier = jnp.where(earlier, 1.0, 0.0).astype(bf16)

    def select_chunk(c, n_eq_before):
        s = sc_scr[c]
        is_eq = s == thr
        eq_f = jnp.where(is_eq, 1.0, 0.0)
        rank = n_eq_before + jnp.dot(earlier, eq_f.astype(bf16), preferred_element_type=f32)
        selected = (s > thr) | (is_eq & (rank < need))
        sc_scr[c] = jnp.transpose(jnp.where(selected, 0.0, MASK_BIAS))
        return n_eq_before + jnp.sum(_fold_rows(jnp.add, eq_f), axis=0, keepdims=True)

    lax.fori_loop(0, n_chunks, select_chunk, jnp.zeros((1, tq), f32))

    q = q_ref[...].astype(f32) * (att_scale * LOG2E)
    q6 = jnp.concatenate([q[:, h * LANE:(h + 1) * LANE] for h in range(SPARSE_HEADS)], axis=0).astype(bf16)

    def scores(c):
        bias = sc_scr[c]
        return _dot_nt(q6, k_ref[key_rows(c), :]) + jnp.concatenate([bias] * SPARSE_HEADS, axis=0)

    m_scr[...] = jnp.full(m_scr.shape, M_INIT, f32)

    def max_pass(c, _):
        m_scr[...] = jnp.maximum(m_scr[...], _fold_lanes(jnp.maximum, scores(c)))
        return 0

    lax.fori_loop(0, n_chunks, max_pass, 0)
    m = jnp.max(m_scr[...], axis=1, keepdims=True)
    m_scr[...] = jnp.broadcast_to(m, m_scr.shape)
    l_scr[...] = jnp.zeros(l_scr.shape, f32)
    acc_scr[...] = jnp.zeros(acc_scr.shape, f32)

    def sum_pass(c, _):
        mb = m_scr[...]
        p = jnp.exp2(scores(c) - jnp.concatenate([mb] * (kc // LANE), axis=1))
        l_scr[...] += _fold_lanes(jnp.add, p)
        acc_scr[...] += jnp.dot(p.astype(bf16), v_ref[key_rows(c), :], preferred_element_type=f32)
        return 0

    lax.fori_loop(0, n_chunks, sum_pass, 0)
    out = acc_scr[...] / jnp.sum(l_scr[...], axis=1, keepdims=True)
    gate = _silu(g_ref[...].astype(f32))
    for h in range(SPARSE_HEADS):
        cols = slice(h * LANE, (h + 1) * LANE)
        o_ref[:, cols] = (out[h * tq:(h + 1) * tq, :] * gate[:, cols]).astype(o_ref.dtype)


def _dsa_mixer(u, bsz, seq):
    tq = min(256, seq)
    nq = seq // tq
    top_k = min(INDEX_TOPK_MAX, seq // 4)
    wide = SPARSE_WIDTH // LANE
    iq_wide = IDX_HEADS * IDX_DIM // LANE
    rows6 = SPARSE_HEADS * tq

    def q_spec(width_units, unit0):
        assert unit0 % width_units == 0
        return pl.BlockSpec((tq, width_units * LANE), lambda b, i: (b * nq + i, unit0 // width_units))

    def kv_spec(unit):
        return pl.BlockSpec((seq, LANE), lambda b, i: (b, unit))

    return pl.pallas_call(
        functools.partial(_dsa_kernel, top_k=top_k),
        grid=(bsz, nq),
        in_specs=[q_spec(wide, U_SQ), q_spec(wide, U_SG), q_spec(iq_wide, U_IQ), q_spec(1, U_IW),
                  kv_spec(U_IK), kv_spec(U_SK), kv_spec(U_SV)],
        out_specs=pl.BlockSpec((tq, SPARSE_WIDTH), lambda b, i: (b * nq + i, 0)),
        out_shape=jax.ShapeDtypeStruct((bsz * seq, SPARSE_WIDTH), jnp.bfloat16),
        scratch_shapes=[pltpu.VMEM((nq, tq, tq), jnp.float32),
                        pltpu.VMEM((rows6, LANE), jnp.float32),
                        pltpu.VMEM((rows6, LANE), jnp.float32),
                        pltpu.VMEM((rows6, SPARSE_HEAD_DIM), jnp.float32)],
        compiler_params=_params("parallel", "arbitrary"),
        name="dsa_mixer",
    )(u, u, u, u, u, u, u)


def _diff_kernel(q_ref, g_ref, k_ref, v_ref, lq1_ref, lk1_ref, lq2_ref, lk2_ref, sg_ref, o_ref,
                 m_scr, l_scr, acc_scr, *, lambda_init):
    tq = q_ref.shape[0]
    kc = tq
    i = pl.program_id(1)
    f32 = jnp.float32
    bf16 = jnp.bfloat16
    scale = DIFF_QK_DIM ** -0.5

    lam = (jnp.exp(jnp.sum(lq1_ref[...] * lk1_ref[...], axis=1, keepdims=True))
           - jnp.exp(jnp.sum(lq2_ref[...] * lk2_ref[...], axis=1, keepdims=True)) + lambda_init)

    lane = lax.broadcasted_iota(jnp.int32, (tq, LANE), 1)
    first = lane < DIFF_QK_DIM
    q2 = []
    for h in range(DIFF_HEADS):
        q = q_ref[:, h * LANE:(h + 1) * LANE].astype(f32) * (scale * LOG2E)
        q2.append(jnp.concatenate([jnp.where(first, q, 0.0), jnp.where(first, 0.0, q)], axis=0).astype(bf16))
    above_diag = (lax.broadcasted_iota(jnp.int32, (tq, kc), 1) > lax.broadcasted_iota(jnp.int32, (tq, kc), 0))
    diag_bias = jnp.where(above_diag, MASK_BIAS, 0.0)
    diag_bias2 = jnp.concatenate([diag_bias, diag_bias], axis=0)

    def key_rows(c):
        return pl.ds(pl.multiple_of(c * kc, kc), kc)

    def scores(c, h, on_diagonal):
        s = _dot_nt(q2[h], k_ref[key_rows(c), h * LANE:(h + 1) * LANE])
        return s + diag_bias2 if on_diagonal else s

    def max_step(c, on_diagonal):
        for h in range(DIFF_HEADS):
            m_scr[h] = jnp.maximum(m_scr[h], _fold_lanes(jnp.maximum, scores(c, h, on_diagonal)))

    def sum_step(c, on_diagonal):
        for h in range(DIFF_HEADS):
            mb = m_scr[h]
            p = jnp.exp2(scores(c, h, on_diagonal) - jnp.concatenate([mb] * (kc // LANE), axis=1))
            l_scr[h] += _fold_lanes(jnp.add, p)
            acc_scr[h] += jnp.dot(p.astype(bf16), v_ref[key_rows(c), h * LANE:(h + 1) * LANE],
                                  preferred_element_type=f32)

    def loop(step):
        def body(c, _):
            step(c, False)
            return 0
        lax.fori_loop(0, i, body, 0)
        step(i, True)

    m_scr[...] = jnp.full(m_scr.shape, M_INIT, f32)
    loop(max_step)
    for h in range(DIFF_HEADS):
        m_scr[h] = jnp.broadcast_to(jnp.max(m_scr[h], axis=1, keepdims=True), m_scr.shape[1:])
    l_scr[...] = jnp.zeros(l_scr.shape, f32)
    acc_scr[...] = jnp.zeros(acc_scr.shape, f32)
    loop(sum_step)

    for h in range(DIFF_HEADS):
        cols = slice(h * LANE, (h + 1) * LANE)
        o = acc_scr[h] / jnp.sum(l_scr[h], axis=1, keepdims=True)
        o = o[:tq, :] - lam * o[tq:, :]
        ms = jnp.sum(o * o, axis=1, keepdims=True) * (1.0 / DIFF_V_DIM)
        o = o * lax.rsqrt(ms + SUBLN_EPS) * sg_ref[...] * (1.0 - lambda_init)
        o_ref[:, cols] = (o * _silu(g_ref[:, cols].astype(f32))).astype(o_ref.dtype)


def _diff_mixer(u, lq1, lk1, lq2, lk2, subln_g, lambda_init, bsz, seq):
    tq = min(256, seq)
    nq = seq // tq
    assert U_DQ % DIFF_HEADS == 0 and U_DG % DIFF_HEADS == 0 and U_DK % DIFF_HEADS == 0 and U_DV % DIFF_HEADS == 0

    def q_spec(unit0):
        return pl.BlockSpec((tq, DIFF_WIDTH), lambda b, i: (b * nq + i, unit0 // DIFF_HEADS))

    def kv_spec(unit0):
        return pl.BlockSpec((seq, DIFF_WIDTH), lambda b, i: (b, unit0 // DIFF_HEADS))

    def vec_spec(n):
        return pl.BlockSpec((1, n), lambda b, i: (0, 0))

    return pl.pallas_call(
        functools.partial(_diff_kernel, lambda_init=lambda_init),
        grid=(bsz, nq),
        in_specs=[q_spec(U_DQ), q_spec(U_DG), kv_spec(U_DK), kv_spec(U_DV),
                  vec_spec(DIFF_QK_DIM), vec_spec(DIFF_QK_DIM), vec_spec(DIFF_QK_DIM), vec_spec(DIFF_QK_DIM),
                  vec_spec(DIFF_V_DIM)],
        out_specs=pl.BlockSpec((tq, DIFF_WIDTH), lambda b, i: (b * nq + i, 0)),
        out_shape=jax.ShapeDtypeStruct((bsz * seq, DIFF_WIDTH), jnp.bfloat16),
        scratch_shapes=[pltpu.VMEM((DIFF_HEADS, 2 * tq, LANE), jnp.float32),
                        pltpu.VMEM((DIFF_HEADS, 2 * tq, LANE), jnp.float32),
                        pltpu.VMEM((DIFF_HEADS, 2 * tq, DIFF_V_DIM), jnp.float32)],
        compiler_params=_params("parallel", "arbitrary"),
        name="diff_mixer",
    )(u, u, u, u, lq1.reshape(1, -1), lk1.reshape(1, -1), lq2.reshape(1, -1), lk2.reshape(1, -1),
      subln_g.reshape(1, -1))


def _outproj_kernel(x_ref, yc_ref, ys_ref, yd_ref, w_ref, g_ref, o_ref, *, final_norm):
    acc = x_ref[...]
    acc = acc + jnp.dot(yc_ref[...], w_ref[0:CONV_WIDTH, :], preferred_element_type=jnp.float32)
    acc = acc + jnp.dot(ys_ref[...], w_ref[CONV_WIDTH:CONV_WIDTH + SPARSE_WIDTH, :],
                        preferred_element_type=jnp.float32)
    acc = acc + jnp.dot(yd_ref[...], w_ref[CONV_WIDTH + SPARSE_WIDTH:, :], preferred_element_type=jnp.float32)
    if final_norm:
        ms = jnp.sum(acc * acc, axis=-1, keepdims=True) * (1.0 / D_MODEL)
        acc = acc * lax.rsqrt(ms + NORM_EPS) * g_ref[...]
    o_ref[...] = acc


def _outproj(x2d, y_conv, y_sparse, y_diff, w_out_bf16, final_g, final_norm):
    m = x2d.shape[0]
    tm = min(512, m)

    def rows(width):
        return pl.BlockSpec((tm, width), lambda i: (i, 0))

    return pl.pallas_call(
        functools.partial(_outproj_kernel, final_norm=final_norm),
        grid=(m // tm,),
        in_specs=[rows(D_MODEL), rows(CONV_WIDTH), rows(SPARSE_WIDTH), rows(DIFF_WIDTH),
                  pl.BlockSpec((D_MODEL, D_MODEL), lambda i: (0, 0)),
                  pl.BlockSpec((1, D_MODEL), lambda i: (0, 0))],
        out_specs=rows(D_MODEL),
        out_shape=jax.ShapeDtypeStruct((m, D_MODEL), jnp.float32),
        compiler_params=_params("parallel"),
        name="outproj",
    )(x2d, y_conv, y_sparse, y_diff, w_out_bf16, final_g.reshape(1, D_MODEL))


def kernel(x, positions, norm_w, w_in, conv_w, lam_q1, lam_k1, lam_q2, lam_k2, subln_w, w_out, final_norm_w):
    bsz, seq, _ = x.shape
    depth = norm_w.shape[0]
    x2d = x.reshape(bsz * seq, D_MODEL)
    tables = _rope_tables(positions)
    for layer in range(depth):
        lambda_init = 0.8 - 0.6 * math.exp(-0.3 * layer)
        u = _inproj(x2d, norm_w[layer], _permute_w_in(w_in[layer]), tables)
        y_conv = _conv_mixer(u, conv_w[layer], bsz, seq)
        y_sparse = _dsa_mixer(u, bsz, seq)
        y_diff = _diff_mixer(u, lam_q1[layer], lam_k1[layer], lam_q2[layer], lam_k2[layer], subln_w[layer],
                             lambda_init, bsz, seq)
        x2d = _outproj(x2d, y_conv, y_sparse, y_diff, w_out[layer].astype(jnp.bfloat16), final_norm_w,
                       final_norm=(layer == depth - 1))
    return x2d.reshape(bsz, seq, D_MODEL)
```

```python
import functools
import math

import jax
import jax.numpy as jnp
import numpy as np
from jax import lax
from jax.experimental import pallas as pl
from jax.experimental.pallas import tpu as pltpu

D_MODEL = 2048
CONV_WIDTH = D_MODEL // 4
CONV_K = 3
SPARSE_HEADS = 6
SPARSE_HEAD_DIM = 128
SPARSE_WIDTH = SPARSE_HEADS * SPARSE_HEAD_DIM
IDX_HEADS = 8
IDX_DIM = 64
INDEX_TOPK_MAX = 256
DIFF_HEADS = 6
DIFF_QK_DIM = 64
DIFF_V_DIM = 2 * DIFF_QK_DIM
DIFF_WIDTH = DIFF_HEADS * DIFF_V_DIM
ROPE_THETA = 10000.0
NORM_EPS = 1e-6
SUBLN_EPS = 1e-5

LANE = 128
SUBLANE = 8
VMEM_LIMIT_BYTES = 56 * 1024 * 1024

U_SQ, U_SG, U_DQ, U_DK, U_DV, U_DG = 0, 6, 12, 18, 24, 30
U_AB, U_AC, U_AH, U_AG = 36, 40, 44, 48
U_IQ, U_SK, U_SV, U_IK, U_IW = 52, 56, 57, 58, 59
N_UNITS = 60
N_COLS = N_UNITS * LANE
UNITS_PER_TILE = 4
T_PLAIN, T_ROPE128, T_ROPE64 = 0, 1, 2
UNIT_TYPES = np.zeros((N_UNITS,), np.int32)
UNIT_TYPES[U_SQ:U_SQ + 6] = T_ROPE128
UNIT_TYPES[U_SK] = T_ROPE128
UNIT_TYPES[U_DQ:U_DQ + 6] = T_ROPE64
UNIT_TYPES[U_DK:U_DK + 6] = T_ROPE64
UNIT_TYPES[U_IQ:U_IQ + 4] = T_ROPE64
UNIT_TYPES[U_IK] = T_ROPE64

M_INIT = -1e30
MASK_BIAS = -2e30
LOG2E = math.log2(math.e)

BISECT_STEPS = 14


def _params(*sem):
    return pltpu.CompilerParams(dimension_semantics=sem, vmem_limit_bytes=VMEM_LIMIT_BYTES)


def _silu(x):
    return x / (1.0 + jnp.exp(-x))


def _dot_nt(a, b):
    return lax.dot_general(a, b, (((1,), (1,)), ((), ())), preferred_element_type=jnp.float32)


def _tree(op, parts):
    parts = list(parts)
    while len(parts) > 1:
        nxt = [op(parts[n], parts[n + 1]) for n in range(0, len(parts) - 1, 2)]
        if len(parts) % 2:
            nxt.append(parts[-1])
        parts = nxt
    return parts[0]


def _fold_lanes(op, x):
    return _tree(op, [x[:, t * LANE:(t + 1) * LANE] for t in range(x.shape[1] // LANE)])


def _fold_rows(op, x):
    return _tree(op, [x[g * SUBLANE:(g + 1) * SUBLANE, :] for g in range(x.shape[0] // SUBLANE)])


TAB_ONE, TAB_ZERO, TAB_C128, TAB_S128, TAB_C64, TAB_SA64, TAB_SB64 = range(7)
N_TABLES = 7
ROPE_COEFS = np.array([[TAB_ONE, TAB_ZERO, TAB_ZERO, TAB_ZERO],
                       [TAB_C128, TAB_S128, TAB_ZERO, TAB_ZERO],
                       [TAB_C64, TAB_ZERO, TAB_SA64, TAB_SB64]],
                      np.int32)
ROPE_SHIFTS = (SPARSE_HEAD_DIM // 2, LANE - IDX_DIM // 2, IDX_DIM // 2)


def _rope_tables_kernel(pos_ref, f128_ref, f64_ref, tab_ref):
    pos = pos_ref[...].astype(jnp.float32)
    shape = tab_ref.shape[1:]
    lane = lax.broadcasted_iota(jnp.int32, shape, 1)
    tab_ref[TAB_ONE] = jnp.ones(shape, jnp.float32)
    tab_ref[TAB_ZERO] = jnp.zeros(shape, jnp.float32)
    ang = pos * f128_ref[...]
    sin = jnp.sin(ang)
    tab_ref[TAB_C128] = jnp.cos(ang)
    tab_ref[TAB_S128] = jnp.where(lane < SPARSE_HEAD_DIM // 2, -sin, sin)
    ang = pos * f64_ref[...]
    sin = jnp.sin(ang)
    first_half = (lane & (IDX_DIM - 1)) < IDX_DIM // 2
    tab_ref[TAB_C64] = jnp.cos(ang)
    tab_ref[TAB_SA64] = jnp.where(first_half, -sin, 0.0)
    tab_ref[TAB_SB64] = jnp.where(first_half, 0.0, sin)


def _rope_tables(positions):
    m = positions.size
    ts = min(1024, m)
    pos = positions.reshape(m, 1)

    def inv_freq(d):
        half = d // 2
        return jnp.exp(-math.log(ROPE_THETA) * jnp.arange(half, dtype=jnp.float32) * (2.0 / d))

    f128 = jnp.tile(inv_freq(SPARSE_HEAD_DIM), LANE // (SPARSE_HEAD_DIM // 2)).reshape(1, LANE)
    f64 = jnp.tile(inv_freq(IDX_DIM), LANE // (IDX_DIM // 2)).reshape(1, LANE)
    const = pl.BlockSpec((1, LANE), lambda i: (0, 0))
    return pl.pallas_call(
        _rope_tables_kernel,
        grid=(m // ts,),
        in_specs=[pl.BlockSpec((ts, 1), lambda i: (i, 0)), const, const],
        out_specs=pl.BlockSpec((N_TABLES, ts, LANE), lambda i: (0, i, 0)),
        out_shape=jax.ShapeDtypeStruct((N_TABLES, m, LANE), jnp.float32),
        compiler_params=_params("parallel"),
        name="rope_tables",
    )(pos, f128, f64)


def _inproj_kernel(types_ref, coefs_ref, x_ref, g_ref, w_ref, tab_ref, o_ref, h_scr, acc_scr, *,
                   row_chunk, n_col_tiles, n_tiles):
    s = pl.program_id(0)
    tm = x_ref.shape[0]

    @pl.when((s % n_col_tiles == 0) & (s < n_tiles))
    def _():
        for r in range(0, tm, row_chunk):
            x = x_ref[r:r + row_chunk, :]
            ms = jnp.sum(x * x, axis=-1, keepdims=True) * (1.0 / D_MODEL)
            h_scr[r:r + row_chunk, :] = (x * lax.rsqrt(ms + NORM_EPS) * g_ref[...]).astype(jnp.bfloat16)

    @pl.when(s == 0)
    def _():
        acc_scr[1] = jnp.zeros(acc_scr.shape[1:], jnp.float32)

    prev = jnp.maximum(s - 1, 0)
    prev_slot = (s + 1) % 2
    for u in range(UNITS_PER_TILE):
        base = ROPE_COEFS.shape[1] * types_ref[(prev % n_col_tiles) * UNITS_PER_TILE + u]
        a = acc_scr[prev_slot, :, u * LANE:(u + 1) * LANE]
        r = a * tab_ref[coefs_ref[base]]
        for n, shift in enumerate(ROPE_SHIFTS):
            r = r + pltpu.roll(a, shift, 1) * tab_ref[coefs_ref[base + 1 + n]]
        o_ref[:, u * LANE:(u + 1) * LANE] = r.astype(o_ref.dtype)

    acc_scr[s % 2] = jnp.dot(h_scr[...], w_ref[...], preferred_element_type=jnp.float32)


def _inproj(x2d, norm_g, w_perm, tables):
    m = x2d.shape[0]
    tm = min(1024, m)
    tn = UNITS_PER_TILE * LANE
    nt = N_COLS // tn
    n_tiles = (m // tm) * nt

    def cur(s):
        return jnp.minimum(s, n_tiles - 1)

    def prev(s):
        return jnp.maximum(s - 1, 0)

    grid_spec = pltpu.PrefetchScalarGridSpec(
        num_scalar_prefetch=2,
        grid=(n_tiles + 1,),
        in_specs=[
            pl.BlockSpec((tm, D_MODEL), lambda s, t, c: (cur(s) // nt, 0)),
            pl.BlockSpec((1, D_MODEL), lambda s, t, c: (0, 0)),
            pl.BlockSpec((D_MODEL, tn), lambda s, t, c: (0, cur(s) % nt)),
            pl.BlockSpec((N_TABLES, tm, LANE), lambda s, t, c: (0, prev(s) // nt, 0)),
        ],
        out_specs=pl.BlockSpec((tm, tn), lambda s, t, c: (prev(s) // nt, prev(s) % nt)),
        scratch_shapes=[pltpu.VMEM((tm, D_MODEL), jnp.bfloat16), pltpu.VMEM((2, tm, tn), jnp.float32)],
    )
    return pl.pallas_call(
        functools.partial(_inproj_kernel, row_chunk=min(256, tm), n_col_tiles=nt, n_tiles=n_tiles),
        grid_spec=grid_spec,
        out_shape=jax.ShapeDtypeStruct((m, N_COLS), jnp.bfloat16),
        compiler_params=_params("arbitrary"),
        name="inproj",
    )(jnp.asarray(UNIT_TYPES), jnp.asarray(ROPE_COEFS.reshape(-1)), x2d, norm_g.reshape(1, D_MODEL), w_perm,
      tables)


def _permute_w_in(w):
    offs = np.cumsum([0, CONV_WIDTH, CONV_WIDTH, CONV_WIDTH, CONV_WIDTH, SPARSE_WIDTH, SPARSE_HEAD_DIM,
                      SPARSE_HEAD_DIM, IDX_HEADS * IDX_DIM, IDX_DIM, IDX_HEADS, SPARSE_WIDTH,
                      2 * DIFF_HEADS * DIFF_QK_DIM, 2 * DIFF_HEADS * DIFF_QK_DIM, DIFF_WIDTH, DIFF_WIDTH])
    w = w.astype(jnp.bfloat16)
    (a_b, a_c, a_h, a_g, s_q, s_k, s_v, i_q, i_k, i_w, s_g, d_q, d_k, d_v, d_g) = [
        w[:, offs[n]:offs[n + 1]] for n in range(15)]
    pad = jnp.zeros((w.shape[0], LANE - IDX_HEADS), w.dtype)
    wp = jnp.concatenate([s_q, s_g, d_q, d_k, d_v, d_g, a_b, a_c, a_h, a_g, i_q, s_k, s_v, i_k, i_k, i_w, pad],
                         axis=1)
    assert wp.shape[1] == N_COLS
    return wp


def _conv_kernel(b_ref, c_ref, h_ref, g_ref, w_ref, o_ref):
    z = c_ref[...].astype(jnp.float32) * h_ref[...].astype(jnp.float32)
    row = lax.broadcasted_iota(jnp.int32, z.shape, 0)
    w = w_ref[...]
    conv = w[CONV_K - 1:CONV_K, :] * z
    for back in range(1, CONV_K):
        shifted = jnp.where(row >= back, pltpu.roll(z, back, 0), 0.0)
        conv = conv + w[CONV_K - 1 - back:CONV_K - back, :] * shifted
    y = b_ref[...].astype(jnp.float32) * conv * _silu(g_ref[...].astype(jnp.float32))
    o_ref[...] = y.astype(o_ref.dtype)


def _conv_mixer(u, conv_w, bsz, seq):
    n_units = CONV_WIDTH // LANE

    def spec(unit0):
        return pl.BlockSpec((seq, LANE), lambda b, c: (b, unit0 + c))

    return pl.pallas_call(
        _conv_kernel,
        grid=(bsz, n_units),
        in_specs=[spec(U_AB), spec(U_AC), spec(U_AH), spec(U_AG),
                  pl.BlockSpec((CONV_K, LANE), lambda b, c: (0, c))],
        out_specs=pl.BlockSpec((seq, LANE), lambda b, c: (b, c)),
        out_shape=jax.ShapeDtypeStruct((bsz * seq, CONV_WIDTH), jnp.bfloat16),
        compiler_params=_params("parallel", "parallel"),
        name="conv_mixer",
    )(u, u, u, u, conv_w)


def _dsa_kernel(q_ref, g_ref, iq_ref, iw_ref, ki_ref, k_ref, v_ref, o_ref, sc_scr, s_scr, m_scr, l_scr, acc_scr,
                *, top_k):
    tq = q_ref.shape[0]
    kc = tq
    i = pl.program_id(1)
    n_chunks = i + 1
    idx_scale = (IDX_DIM * IDX_HEADS) ** -0.5
    att_scale = SPARSE_HEAD_DIM ** -0.5
    f32 = jnp.float32
    bf16 = jnp.bfloat16
    top_kf = float(top_k)

    def key_rows(c):
        return pl.ds(pl.multiple_of(c * kc, kc), kc)

    lane = lax.broadcasted_iota(jnp.int32, (tq, LANE), 1)
    low_half = lane < IDX_DIM
    q_heads = []
    for unit in range(IDX_HEADS * IDX_DIM // LANE):
        qu = iq_ref[:, unit * LANE:(unit + 1) * LANE].astype(f32)
        q_heads.append(jnp.where(low_half, qu, 0.0).astype(bf16))
        q_heads.append(jnp.where(low_half, 0.0, qu).astype(bf16))
    w_t = jnp.transpose(iw_ref[...].astype(f32)) * idx_scale
    w_rows = [w_t[h:h + 1, :] for h in range(IDX_HEADS)]
    above_diag = (lax.broadcasted_iota(jnp.int32, (kc, tq), 0) > lax.broadcasted_iota(jnp.int32, (kc, tq), 1))

    def index_chunk(c, carry, on_diagonal):
        rmax8, rmin8 = carry
        kic = ki_ref[key_rows(c), :]
        score = w_rows[0] * jnp.maximum(_dot_nt(kic, q_heads[0]), 0.0)
        for h in range(1, IDX_HEADS):
            score = score + w_rows[h] * jnp.maximum(_dot_nt(kic, q_heads[h]), 0.0)
        if on_diagonal:
            hi_part = jnp.where(above_diag, -jnp.inf, score)
            lo_part = jnp.where(above_diag, jnp.inf, score)
        else:
            hi_part = lo_part = score
        sc_scr[c] = hi_part
        return (jnp.maximum(rmax8, _fold_rows(jnp.maximum, hi_part)),
                jnp.minimum(rmin8, _fold_rows(jnp.minimum, lo_part)))

    carry = (jnp.full((SUBLANE, tq), -jnp.inf, f32), jnp.full((SUBLANE, tq), jnp.inf, f32))
    carry = lax.fori_loop(0, i, lambda c, cr: index_chunk(c, cr, False), carry)
    rmax8, rmin8 = index_chunk(i, carry, True)
    rmax = jnp.max(rmax8, axis=0, keepdims=True)
    rmin = jnp.min(rmin8, axis=0, keepdims=True)

    def count(pred):
        def body(c, acc8):
            return acc8 + _fold_rows(jnp.add, jnp.where(pred(sc_scr[c]), 1.0, 0.0))
        acc8 = lax.fori_loop(0, n_chunks, body, jnp.zeros((SUBLANE, tq), f32))
        return jnp.sum(acc8, axis=0, keepdims=True)

    def min_where(pred):
        def body(c, acc8):
            s = sc_scr[c]
            return jnp.minimum(acc8, _fold_rows(jnp.minimum, jnp.where(pred(s), s, jnp.inf)))
        acc8 = lax.fori_loop(0, n_chunks, body, jnp.full((SUBLANE, tq), jnp.inf, f32))
        return jnp.min(acc8, axis=0, keepdims=True)

    def bisect(_, carry):
        lo, hi = carry
        p = 0.5 * (lo + hi)
        enough = count(lambda s: s >= p) >= top_kf
        return jnp.where(enough, p, lo), jnp.where(enough, hi, p)

    lo, _ = lax.fori_loop(0, BISECT_STEPS, bisect, (rmin, rmax))
    thr = min_where(lambda s: s >= lo)
    n_gt = count(lambda s: s > thr)

    def unsettled(carry):
        return jnp.max(carry[1]) >= top_kf

    def walk(carry):
        thr, n_gt = carry
        nxt = min_where(lambda s: s > thr)
        n_nxt = count(lambda s: s > nxt)
        move = n_gt >= top_kf
        return jnp.where(move, nxt, thr), jnp.where(move, n_nxt, n_gt)

    thr, n_gt = lax.while_loop(unsettled, walk, (thr, n_gt))
    need = top_kf - n_gt

    earlier = (lax.broadcasted_iota(jnp.int32, (kc, kc), 1) < lax.broadcasted_iota(jnp.int32, (kc, kc), 0))
    earlier = jnp.where(earlier, 1.0, 0.0).astype(bf16)
    q = q_ref[...].astype(f32) * (att_scale * LOG2E)
    q6 = jnp.concatenate([q[:, h * LANE:(h + 1) * LANE] for h in range(SPARSE_HEADS)], axis=0).astype(bf16)
    m_scr[...] = jnp.full(m_scr.shape, M_INIT, f32)

    def score_pass(c, n_eq_before):
        s_idx = sc_scr[c]
        is_eq = s_idx == thr
        eq_f = jnp.where(is_eq, 1.0, 0.0)
        rank = n_eq_before + jnp.dot(earlier, eq_f.astype(bf16), preferred_element_type=f32)
        selected = (s_idx > thr) | (is_eq & (rank < need))
        bias = jnp.transpose(jnp.where(selected, 0.0, MASK_BIAS))
        s = _dot_nt(q6, k_ref[key_rows(c), :]) + jnp.concatenate([bias] * SPARSE_HEADS, axis=0)
        s_scr[c] = s
        m_scr[...] = jnp.maximum(m_scr[...], _fold_lanes(jnp.maximum, s))
        return n_eq_before + jnp.sum(_fold_rows(jnp.add, eq_f), axis=0, keepdims=True)

    lax.fori_loop(0, n_chunks, score_pass, jnp.zeros((1, tq), f32))
    m = jnp.max(m_scr[...], axis=1, keepdims=True)
    m_scr[...] = jnp.broadcast_to(m, m_scr.shape)
    l_scr[...] = jnp.zeros(l_scr.shape, f32)
    acc_scr[...] = jnp.zeros(acc_scr.shape, f32)

    def sum_pass(c, _):
        mb = m_scr[...]
        p = jnp.exp2(s_scr[c] - jnp.concatenate([mb] * (kc // LANE), axis=1))
        l_scr[...] += _fold_lanes(jnp.add, p)
        acc_scr[...] += jnp.dot(p.astype(bf16), v_ref[key_rows(c), :], preferred_element_type=f32)
        return 0

    lax.fori_loop(0, n_chunks, sum_pass, 0)
    out = acc_scr[...] / jnp.sum(l_scr[...], axis=1, keepdims=True)
    gate = _silu(g_ref[...].astype(f32))
    for h in range(SPARSE_HEADS):
        cols = slice(h * LANE, (h + 1) * LANE)
        o_ref[:, cols] = (out[h * tq:(h + 1) * tq, :] * gate[:, cols]).astype(o_ref.dtype)


def _dsa_mixer(u, bsz, seq):
    tq = min(256, seq)
    nq = seq // tq
    top_k = min(INDEX_TOPK_MAX, seq // 4)
    wide = SPARSE_WIDTH // LANE
    iq_wide = IDX_HEADS * IDX_DIM // LANE
    rows6 = SPARSE_HEADS * tq

    def q_spec(width_units, unit0):
        assert unit0 % width_units == 0
        return pl.BlockSpec((tq, width_units * LANE), lambda b, i: (b * nq + i, unit0 // width_units))

    def kv_spec(unit):
        return pl.BlockSpec((seq, LANE), lambda b, i: (b, unit))

    return pl.pallas_call(
        functools.partial(_dsa_kernel, top_k=top_k),
        grid=(bsz, nq),
        in_specs=[q_spec(wide, U_SQ), q_spec(wide, U_SG), q_spec(iq_wide, U_IQ), q_spec(1, U_IW),
                  kv_spec(U_IK), kv_spec(U_SK), kv_spec(U_SV)],
        out_specs=pl.BlockSpec((tq, SPARSE_WIDTH), lambda b, i: (b * nq + i, 0)),
        out_shape=jax.ShapeDtypeStruct((bsz * seq, SPARSE_WIDTH), jnp.bfloat16),
        scratch_shapes=[pltpu.VMEM((nq, tq, tq), jnp.float32),
                        pltpu.VMEM((nq, rows6, tq), jnp.float32),
                        pltpu.VMEM((rows6, LANE), jnp.float32),
                        pltpu.VMEM((rows6, LANE), jnp.float32),
                        pltpu.VMEM((rows6, SPARSE_HEAD_DIM), jnp.float32)],
        compiler_params=_params("parallel", "arbitrary"),
        name="dsa_mixer",
    )(u, u, u, u, u, u, u)


def _diff_kernel(q_ref, g_ref, k_ref, v_ref, lq1_ref, lk1_ref, lq2_ref, lk2_ref, sg_ref, o_ref,
                 m_scr, l_scr, acc_scr, *, lambda_init):
    tq = q_ref.shape[0]
    kc = tq
    i = pl.program_id(1)
    f32 = jnp.float32
    bf16 = jnp.bfloat16
    scale = DIFF_QK_DIM ** -0.5

    lam = (jnp.exp(jnp.sum(lq1_ref[...] * lk1_ref[...], axis=1, keepdims=True))
           - jnp.exp(jnp.sum(lq2_ref[...] * lk2_ref[...], axis=1, keepdims=True)) + lambda_init)

    lane = lax.broadcasted_iota(jnp.int32, (tq, LANE), 1)
    first = lane < DIFF_QK_DIM
    q2 = []
    for h in range(DIFF_HEADS):
        q = q_ref[:, h * LANE:(h + 1) * LANE].astype(f32) * (scale * LOG2E)
        q2.append(jnp.concatenate([jnp.where(first, q, 0.0), jnp.where(first, 0.0, q)], axis=0).astype(bf16))
    above_diag = (lax.broadcasted_iota(jnp.int32, (tq, kc), 1) > lax.broadcasted_iota(jnp.int32, (tq, kc), 0))
    diag_bias = jnp.where(above_diag, MASK_BIAS, 0.0)
    diag_bias2 = jnp.concatenate([diag_bias, diag_bias], axis=0)

    def key_rows(c):
        return pl.ds(pl.multiple_of(c * kc, kc), kc)

    def scores(c, h, on_diagonal):
        s = _dot_nt(q2[h], k_ref[key_rows(c), h * LANE:(h + 1) * LANE])
        return s + diag_bias2 if on_diagonal else s

    def max_step(c, on_diagonal):
        for h in range(DIFF_HEADS):
            m_scr[h] = jnp.maximum(m_scr[h], _fold_lanes(jnp.maximum, scores(c, h, on_diagonal)))

    def sum_step(c, on_diagonal):
        for h in range(DIFF_HEADS):
            mb = m_scr[h]
            p = jnp.exp2(scores(c, h, on_diagonal) - jnp.concatenate([mb] * (kc // LANE), axis=1))
            l_scr[h] += _fold_lanes(jnp.add, p)
            acc_scr[h] += jnp.dot(p.astype(bf16), v_ref[key_rows(c), h * LANE:(h + 1) * LANE],
                                  preferred_element_type=f32)

    def loop(step):
        def body(c, _):
            step(c, False)
            return 0
        lax.fori_loop(0, i, body, 0)
        step(i, True)

    m_scr[...] = jnp.full(m_scr.shape, M_INIT, f32)
    loop(max_step)
    for h in range(DIFF_HEADS):
        m_scr[h] = jnp.broadcast_to(jnp.max(m_scr[h], axis=1, keepdims=True), m_scr.shape[1:])
    l_scr[...] = jnp.zeros(l_scr.shape, f32)
    acc_scr[...] = jnp.zeros(acc_scr.shape, f32)
    loop(sum_step)

    for h in range(DIFF_HEADS):
        cols = slice(h * LANE, (h + 1) * LANE)
        o = acc_scr[h] / jnp.sum(l_scr[h], axis=1, keepdims=True)
        o = o[:tq, :] - lam * o[tq:, :]
        ms = jnp.sum(o * o, axis=1, keepdims=True) * (1.0 / DIFF_V_DIM)
        o = o * lax.rsqrt(ms + SUBLN_EPS) * sg_ref[...] * (1.0 - lambda_init)
        o_ref[:, cols] = (o * _silu(g_ref[:, cols].astype(f32))).astype(o_ref.dtype)


def _diff_mixer(u, lq1, lk1, lq2, lk2, subln_g, lambda_init, bsz, seq):
    tq = min(256, seq)
    nq = seq // tq
    assert U_DQ % DIFF_HEADS == 0 and U_DG % DIFF_HEADS == 0 and U_DK % DIFF_HEADS == 0 and U_DV % DIFF_HEADS == 0

    def q_spec(unit0):
        return pl.BlockSpec((tq, DIFF_WIDTH), lambda b, i: (b * nq + i, unit0 // DIFF_HEADS))

    def kv_spec(unit0):
        return pl.BlockSpec((seq, DIFF_WIDTH), lambda b, i: (b, unit0 // DIFF_HEADS))

    def vec_spec(n):
        return pl.BlockSpec((1, n), lambda b, i: (0, 0))

    return pl.pallas_call(
        functools.partial(_diff_kernel, lambda_init=lambda_init),
        grid=(bsz, nq),
        in_specs=[q_spec(U_DQ), q_spec(U_DG), kv_spec(U_DK), kv_spec(U_DV),
                  vec_spec(DIFF_QK_DIM), vec_spec(DIFF_QK_DIM), vec_spec(DIFF_QK_DIM), vec_spec(DIFF_QK_DIM),
                  vec_spec(DIFF_V_DIM)],
        out_specs=pl.BlockSpec((tq, DIFF_WIDTH), lambda b, i: (b * nq + i, 0)),
        out_shape=jax.ShapeDtypeStruct((bsz * seq, DIFF_WIDTH), jnp.bfloat16),
        scratch_shapes=[pltpu.VMEM((DIFF_HEADS, 2 * tq, LANE), jnp.float32),
                        pltpu.VMEM((DIFF_HEADS, 2 * tq, LANE), jnp.float32),
                        pltpu.VMEM((DIFF_HEADS, 2 * tq, DIFF_V_DIM), jnp.float32)],
        compiler_params=_params("parallel", "arbitrary"),
        name="diff_mixer",
    )(u, u, u, u, lq1.reshape(1, -1), lk1.reshape(1, -1), lq2.reshape(1, -1), lk2.reshape(1, -1),
      subln_g.reshape(1, -1))


def _outproj_kernel(x_ref, yc_ref, ys_ref, yd_ref, w_ref, g_ref, o_ref, *, final_norm):
    acc = x_ref[...]
    acc = acc + jnp.dot(yc_ref[...], w_ref[0:CONV_WIDTH, :], preferred_element_type=jnp.float32)
    acc = acc + jnp.dot(ys_ref[...], w_ref[CONV_WIDTH:CONV_WIDTH + SPARSE_WIDTH, :],
                        preferred_element_type=jnp.float32)
    acc = acc + jnp.dot(yd_ref[...], w_ref[CONV_WIDTH + SPARSE_WIDTH:, :], preferred_element_type=jnp.float32)
    if final_norm:
        ms = jnp.sum(acc * acc, axis=-1, keepdims=True) * (1.0 / D_MODEL)
        acc = acc * lax.rsqrt(ms + NORM_EPS) * g_ref[...]
    o_ref[...] = acc


def _outproj(x2d, y_conv, y_sparse, y_diff, w_out_bf16, final_g, final_norm):
    m = x2d.shape[0]
    tm = min(512, m)

    def rows(width):
        return pl.BlockSpec((tm, width), lambda i: (i, 0))

    return pl.pallas_call(
        functools.partial(_outproj_kernel, final_norm=final_norm),
        grid=(m // tm,),
        in_specs=[rows(D_MODEL), rows(CONV_WIDTH), rows(SPARSE_WIDTH), rows(DIFF_WIDTH),
                  pl.BlockSpec((D_MODEL, D_MODEL), lambda i: (0, 0)),
                  pl.BlockSpec((1, D_MODEL), lambda i: (0, 0))],
        out_specs=rows(D_MODEL),
        out_shape=jax.ShapeDtypeStruct((m, D_MODEL), jnp.float32),
        compiler_params=_params("parallel"),
        name="outproj",
    )(x2d, y_conv, y_sparse, y_diff, w_out_bf16, final_g.reshape(1, D_MODEL))


def kernel(x, positions, norm_w, w_in, conv_w, lam_q1, lam_k1, lam_q2, lam_k2, subln_w, w_out, final_norm_w):
    bsz, seq, _ = x.shape
    depth = norm_w.shape[0]
    x2d = x.reshape(bsz * seq, D_MODEL)
    tables = _rope_tables(positions)
    for layer in range(depth):
        lambda_init = 0.8 - 0.6 * math.exp(-0.3 * layer)
        u = _inproj(x2d, norm_w[layer], _permute_w_in(w_in[layer]), tables)
        y_conv = _conv_mixer(u, conv_w[layer], bsz, seq)
        y_sparse = _dsa_mixer(u, bsz, seq)
        y_diff = _diff_mixer(u, lam_q1[layer], lam_k1[layer], lam_q2[layer], lam_k2[layer], subln_w[layer],
                             lambda_init, bsz, seq)
        x2d = _outproj(x2d, y_conv, y_sparse, y_diff, w_out[layer].astype(jnp.bfloat16), final_norm_w,
                       final_norm=(layer == depth - 1))
    return x2d.reshape(bsz, seq, D_MODEL)
```

```python
import functools
import math

import jax
import jax.numpy as jnp
import numpy as np
from jax import lax
from jax.experimental import pallas as pl
from jax.experimental.pallas import tpu as pltpu

D_MODEL = 2048
CONV_WIDTH = D_MODEL // 4
CONV_K = 3
SPARSE_HEADS = 6
SPARSE_HEAD_DIM = 128
SPARSE_WIDTH = SPARSE_HEADS * SPARSE_HEAD_DIM
IDX_HEADS = 8
IDX_DIM = 64
INDEX_TOPK_MAX = 256
DIFF_HEADS = 6
DIFF_QK_DIM = 64
DIFF_V_DIM = 2 * DIFF_QK_DIM
DIFF_WIDTH = DIFF_HEADS * DIFF_V_DIM
ROPE_THETA = 10000.0
NORM_EPS = 1e-6
SUBLN_EPS = 1e-5

LANE = 128
SUBLANE = 8
VMEM_LIMIT_BYTES = 56 * 1024 * 1024

U_SQ, U_SG, U_DQ, U_DK, U_DV, U_DG = 0, 6, 12, 18, 24, 30
U_AB, U_AC, U_AH, U_AG = 36, 40, 44, 48
U_IQ, U_SK, U_SV, U_IK, U_IW = 52, 56, 57, 58, 59
N_UNITS = 60
N_COLS = N_UNITS * LANE
UNITS_PER_TILE = 4
T_PLAIN, T_ROPE128, T_ROPE64 = 0, 1, 2
UNIT_TYPES = np.zeros((N_UNITS,), np.int32)
UNIT_TYPES[U_SQ:U_SQ + 6] = T_ROPE128
UNIT_TYPES[U_SK] = T_ROPE128
UNIT_TYPES[U_DQ:U_DQ + 6] = T_ROPE64
UNIT_TYPES[U_DK:U_DK + 6] = T_ROPE64
UNIT_TYPES[U_IQ:U_IQ + 4] = T_ROPE64
UNIT_TYPES[U_IK] = T_ROPE64

M_INIT = -1e30
MASK_BIAS = -2e30
LOG2E = math.log2(math.e)

BISECT_STEPS = 14


def _params(*sem):
    return pltpu.CompilerParams(dimension_semantics=sem, vmem_limit_bytes=VMEM_LIMIT_BYTES)


def _silu(x):
    return x / (1.0 + jnp.exp(-x))


def _dot_nt(a, b):
    return lax.dot_general(a, b, (((1,), (1,)), ((), ())), preferred_element_type=jnp.float32)


def _tree(op, parts):
    parts = list(parts)
    while len(parts) > 1:
        nxt = [op(parts[n], parts[n + 1]) for n in range(0, len(parts) - 1, 2)]
        if len(parts) % 2:
            nxt.append(parts[-1])
        parts = nxt
    return parts[0]


def _fold_lanes(op, x):
    return _tree(op, [x[:, t * LANE:(t + 1) * LANE] for t in range(x.shape[1] // LANE)])


def _fold_rows(op, x):
    return _tree(op, [x[g * SUBLANE:(g + 1) * SUBLANE, :] for g in range(x.shape[0] // SUBLANE)])


TAB_ONE, TAB_ZERO, TAB_C128, TAB_S128, TAB_C64, TAB_S64 = range(6)
N_TABLES = 6
ROPE_COEFS = np.array([[TAB_ONE, TAB_ZERO],
                       [TAB_C128, TAB_S128],
                       [TAB_C64, TAB_S64]],
                      np.int32)
ROPE_SHIFT = LANE // 2


def _rope_tables_kernel(pos_ref, f128_ref, f64_ref, tab_ref):
    pos = pos_ref[...].astype(jnp.float32)
    shape = tab_ref.shape[1:]
    first_half = lax.broadcasted_iota(jnp.int32, shape, 1) < ROPE_SHIFT
    tab_ref[TAB_ONE] = jnp.ones(shape, jnp.float32)
    tab_ref[TAB_ZERO] = jnp.zeros(shape, jnp.float32)
    ang = pos * f128_ref[...]
    sin = jnp.sin(ang)
    tab_ref[TAB_C128] = jnp.cos(ang)
    tab_ref[TAB_S128] = jnp.where(first_half, -sin, sin)
    ang = pos * f64_ref[...]
    sin = jnp.sin(ang)
    tab_ref[TAB_C64] = jnp.cos(ang)
    tab_ref[TAB_S64] = jnp.where(first_half, -sin, sin)


def _rope_tables(positions):
    m = positions.size
    ts = min(1024, m)
    pos = positions.reshape(m, 1)

    def inv_freq(d):
        half = d // 2
        return jnp.exp(-math.log(ROPE_THETA) * jnp.arange(half, dtype=jnp.float32) * (2.0 / d))

    f128 = jnp.tile(inv_freq(SPARSE_HEAD_DIM), LANE // (SPARSE_HEAD_DIM // 2)).reshape(1, LANE)
    f64 = jnp.tile(inv_freq(IDX_DIM), LANE // (IDX_DIM // 2)).reshape(1, LANE)
    const = pl.BlockSpec((1, LANE), lambda i: (0, 0))
    return pl.pallas_call(
        _rope_tables_kernel,
        grid=(m // ts,),
        in_specs=[pl.BlockSpec((ts, 1), lambda i: (i, 0)), const, const],
        out_specs=pl.BlockSpec((N_TABLES, ts, LANE), lambda i: (0, i, 0)),
        out_shape=jax.ShapeDtypeStruct((N_TABLES, m, LANE), jnp.float32),
        compiler_params=_params("parallel"),
        name="rope_tables",
    )(pos, f128, f64)


def _inproj_kernel(types_ref, coefs_ref, x_ref, g_ref, w_ref, tab_ref, o_ref, h_scr, acc_scr, *,
                   row_chunk, n_col_tiles, n_tiles):
    s = pl.program_id(0)
    tm = x_ref.shape[0]

    @pl.when((s % n_col_tiles == 0) & (s < n_tiles))
    def _():
        for r in range(0, tm, row_chunk):
            x = x_ref[r:r + row_chunk, :]
            ms = jnp.sum(x * x, axis=-1, keepdims=True) * (1.0 / D_MODEL)
            h_scr[r:r + row_chunk, :] = (x * lax.rsqrt(ms + NORM_EPS) * g_ref[...]).astype(jnp.bfloat16)

    @pl.when(s == 0)
    def _():
        acc_scr[1] = jnp.zeros(acc_scr.shape[1:], jnp.float32)

    prev = jnp.maximum(s - 1, 0)
    prev_slot = (s + 1) % 2
    for u in range(UNITS_PER_TILE):
        base = ROPE_COEFS.shape[1] * types_ref[(prev % n_col_tiles) * UNITS_PER_TILE + u]
        a = acc_scr[prev_slot, :, u * LANE:(u + 1) * LANE]
        r = a * tab_ref[coefs_ref[base]] + pltpu.roll(a, ROPE_SHIFT, 1) * tab_ref[coefs_ref[base + 1]]
        o_ref[:, u * LANE:(u + 1) * LANE] = r.astype(o_ref.dtype)

    acc_scr[s % 2] = jnp.dot(h_scr[...], w_ref[...], preferred_element_type=jnp.float32)


def _inproj(x2d, norm_g, w_prep, layer, tables):
    m = x2d.shape[0]
    tm = min(1024, m)
    tn = UNITS_PER_TILE * LANE
    nt = N_COLS // tn
    n_tiles = (m // tm) * nt

    def cur(s):
        return jnp.minimum(s, n_tiles - 1)

    def prev(s):
        return jnp.maximum(s - 1, 0)

    grid_spec = pltpu.PrefetchScalarGridSpec(
        num_scalar_prefetch=2,
        grid=(n_tiles + 1,),
        in_specs=[
            pl.BlockSpec((tm, D_MODEL), lambda s, t, c: (cur(s) // nt, 0)),
            pl.BlockSpec((1, D_MODEL), lambda s, t, c: (0, 0)),
            pl.BlockSpec((None, D_MODEL, tn), lambda s, t, c: (layer, 0, cur(s) % nt)),
            pl.BlockSpec((N_TABLES, tm, LANE), lambda s, t, c: (0, prev(s) // nt, 0)),
        ],
        out_specs=pl.BlockSpec((tm, tn), lambda s, t, c: (prev(s) // nt, prev(s) % nt)),
        scratch_shapes=[pltpu.VMEM((tm, D_MODEL), jnp.bfloat16), pltpu.VMEM((2, tm, tn), jnp.float32)],
    )
    return pl.pallas_call(
        functools.partial(_inproj_kernel, row_chunk=min(256, tm), n_col_tiles=nt, n_tiles=n_tiles),
        grid_spec=grid_spec,
        out_shape=jax.ShapeDtypeStruct((m, N_COLS), jnp.bfloat16),
        compiler_params=_params("arbitrary"),
        name="inproj",
    )(jnp.asarray(UNIT_TYPES), jnp.asarray(ROPE_COEFS.reshape(-1)), x2d, norm_g.reshape(1, D_MODEL), w_prep,
      tables)


def _w_in_copy_plan():
    widths = [CONV_WIDTH, CONV_WIDTH, CONV_WIDTH, CONV_WIDTH, SPARSE_WIDTH, SPARSE_HEAD_DIM, SPARSE_HEAD_DIM,
              IDX_HEADS * IDX_DIM, IDX_DIM, IDX_HEADS, SPARSE_WIDTH, 2 * DIFF_HEADS * DIFF_QK_DIM,
              2 * DIFF_HEADS * DIFF_QK_DIM, DIFF_WIDTH, DIFF_WIDTH]
    names = ["a_b", "a_c", "a_h", "a_g", "s_q", "s_k", "s_v", "i_q", "i_k", "i_w", "s_g", "d_q", "d_k", "d_v", "d_g"]
    src = dict(zip(names, np.cumsum([0] + widths[:-1]).tolist()))
    width = dict(zip(names, widths))
    plan = []
    for name, unit in [("s_q", U_SQ), ("s_g", U_SG), ("d_v", U_DV), ("d_g", U_DG), ("a_b", U_AB), ("a_c", U_AC),
                       ("a_h", U_AH), ("a_g", U_AG), ("s_k", U_SK), ("s_v", U_SV), ("i_w", U_IW)]:
        plan.append((unit * LANE, src[name], width[name]))
    half = IDX_DIM // 2

    def interleave(dst_unit, head_a, head_b):
        for part, head in enumerate((head_a, head_b, head_a, head_b)):
            plan.append((dst_unit * LANE + part * half, head + (part // 2) * half, half))

    for name, unit in [("d_q", U_DQ), ("d_k", U_DK), ("i_q", U_IQ)]:
        for n in range(width[name] // LANE):
            interleave(unit + n, src[name] + n * LANE, src[name] + n * LANE + IDX_DIM)
    interleave(U_IK, src["i_k"], src["i_k"])
    return plan, sum(widths)


def _prep_w_in_kernel(w_ref, o_ref, *, plan):
    rows = w_ref.shape[1]
    pad0 = U_IW * LANE + IDX_HEADS
    o_ref[0, :, pad0:(U_IW + 1) * LANE] = jnp.zeros((rows, (U_IW + 1) * LANE - pad0), o_ref.dtype)
    for dst, src, width in plan:
        o_ref[0, :, dst:dst + width] = w_ref[0, :, src:src + width].astype(o_ref.dtype)


def _prep_w_in(w_in):
    depth, rows, n_in = w_in.shape
    plan, n_src = _w_in_copy_plan()
    assert n_src == n_in
    tr = 256
    return pl.pallas_call(
        functools.partial(_prep_w_in_kernel, plan=plan),
        grid=(depth, rows // tr),
        in_specs=[pl.BlockSpec((1, tr, n_in), lambda l, r: (l, r, 0))],
        out_specs=pl.BlockSpec((1, tr, N_COLS), lambda l, r: (l, r, 0)),
        out_shape=jax.ShapeDtypeStruct((depth, rows, N_COLS), jnp.bfloat16),
        compiler_params=_params("parallel", "parallel"),
        name="prep_w_in",
    )(w_in)


def _conv_kernel(b_ref, c_ref, h_ref, g_ref, w_ref, o_ref):
    z = c_ref[...].astype(jnp.float32) * h_ref[...].astype(jnp.float32)
    row = lax.broadcasted_iota(jnp.int32, z.shape, 0)
    w = w_ref[...]
    conv = w[CONV_K - 1:CONV_K, :] * z
    for back in range(1, CONV_K):
        shifted = jnp.where(row >= back, pltpu.roll(z, back, 0), 0.0)
        conv = conv + w[CONV_K - 1 - back:CONV_K - back, :] * shifted
    y = b_ref[...].astype(jnp.float32) * conv * _silu(g_ref[...].astype(jnp.float32))
    o_ref[...] = y.astype(o_ref.dtype)


def _conv_mixer(u, conv_w, bsz, seq):
    n_units = CONV_WIDTH // LANE

    def spec(unit0):
        return pl.BlockSpec((seq, LANE), lambda b, c: (b, unit0 + c))

    return pl.pallas_call(
        _conv_kernel,
        grid=(bsz, n_units),
        in_specs=[spec(U_AB), spec(U_AC), spec(U_AH), spec(U_AG),
                  pl.BlockSpec((CONV_K, LANE), lambda b, c: (0, c))],
        out_specs=pl.BlockSpec((seq, LANE), lambda b, c: (b, c)),
        out_shape=jax.ShapeDtypeStruct((bsz * seq, CONV_WIDTH), jnp.bfloat16),
        compiler_params=_params("parallel", "parallel"),
        name="conv_mixer",
    )(u, u, u, u, conv_w)


def _dsa_kernel(q_ref, g_ref, iq_ref, iw_ref, ki_ref, k_ref, v_ref, o_ref, sc_scr, s_scr, m_scr, l_scr, acc_scr,
                *, top_k):
    tq = q_ref.shape[0]
    kc = tq
    i = pl.program_id(1)
    n_chunks = i + 1
    idx_scale = (IDX_DIM * IDX_HEADS) ** -0.5
    att_scale = SPARSE_HEAD_DIM ** -0.5
    f32 = jnp.float32
    bf16 = jnp.bfloat16
    top_kf = float(top_k)

    def key_rows(c):
        return pl.ds(pl.multiple_of(c * kc, kc), kc)

    lane = lax.broadcasted_iota(jnp.int32, (tq, LANE), 1)
    low_half = (lane & (IDX_DIM - 1)) < IDX_DIM // 2
    q_heads = []
    for unit in range(IDX_HEADS * IDX_DIM // LANE):
        qu = iq_ref[:, unit * LANE:(unit + 1) * LANE].astype(f32)
        q_heads.append(jnp.where(low_half, qu, 0.0).astype(bf16))
        q_heads.append(jnp.where(low_half, 0.0, qu).astype(bf16))
    w_t = jnp.transpose(iw_ref[...].astype(f32)) * idx_scale
    w_rows = [w_t[h:h + 1, :] for h in range(IDX_HEADS)]
    above_diag = (lax.broadcasted_iota(jnp.int32, (kc, tq), 0) > lax.broadcasted_iota(jnp.int32, (kc, tq), 1))

    def index_chunk(c, carry, on_diagonal):
        rmax8, rmin8 = carry
        kic = ki_ref[key_rows(c), :]
        score = w_rows[0] * jnp.maximum(_dot_nt(kic, q_heads[0]), 0.0)
        for h in range(1, IDX_HEADS):
            score = score + w_rows[h] * jnp.maximum(_dot_nt(kic, q_heads[h]), 0.0)
        if on_diagonal:
            hi_part = jnp.where(above_diag, -jnp.inf, score)
            lo_part = jnp.where(above_diag, jnp.inf, score)
        else:
            hi_part = lo_part = score
        sc_scr[c] = hi_part
        return (jnp.maximum(rmax8, _fold_rows(jnp.maximum, hi_part)),
                jnp.minimum(rmin8, _fold_rows(jnp.minimum, lo_part)))

    carry = (jnp.full((SUBLANE, tq), -jnp.inf, f32), jnp.full((SUBLANE, tq), jnp.inf, f32))
    carry = lax.fori_loop(0, i, lambda c, cr: index_chunk(c, cr, False), carry)
    rmax8, rmin8 = index_chunk(i, carry, True)
    rmax = jnp.max(rmax8, axis=0, keepdims=True)
    rmin = jnp.min(rmin8, axis=0, keepdims=True)

    def count(pred):
        def body(c, acc8):
            return acc8 + _fold_rows(jnp.add, jnp.where(pred(sc_scr[c]), 1.0, 0.0))
        acc8 = lax.fori_loop(0, n_chunks, body, jnp.zeros((SUBLANE, tq), f32))
        return jnp.sum(acc8, axis=0, keepdims=True)

    def min_where(pred):
        def body(c, acc8):
            s = sc_scr[c]
            return jnp.minimum(acc8, _fold_rows(jnp.minimum, jnp.where(pred(s), s, jnp.inf)))
        acc8 = lax.fori_loop(0, n_chunks, body, jnp.full((SUBLANE, tq), jnp.inf, f32))
        return jnp.min(acc8, axis=0, keepdims=True)

    def bisect(_, carry):
        lo, hi = carry
        p = 0.5 * (lo + hi)
        enough = count(lambda s: s >= p) >= top_kf
        return jnp.where(enough, p, lo), jnp.where(enough, hi, p)

    lo, _ = lax.fori_loop(0, BISECT_STEPS, bisect, (rmin, rmax))
    thr = min_where(lambda s: s >= lo)
    n_gt = count(lambda s: s > thr)

    def unsettled(carry):
        return jnp.max(carry[1]) >= top_kf

    def walk(carry):
        thr, n_gt = carry
        nxt = min_where(lambda s: s > thr)
        n_nxt = count(lambda s: s > nxt)
        move = n_gt >= top_kf
        return jnp.where(move, nxt, thr), jnp.where(move, n_nxt, n_gt)

    thr, n_gt = lax.while_loop(unsettled, walk, (thr, n_gt))
    need = top_kf - n_gt

    earlier = (lax.broadcasted_iota(jnp.int32, (kc, kc), 1) < lax.broadcasted_iota(jnp.int32, (kc, kc), 0))
    earlier = jnp.where(earlier, 1.0, 0.0).astype(bf16)
    q = q_ref[...].astype(f32) * (att_scale * LOG2E)
    q6 = jnp.concatenate([q[:, h * LANE:(h + 1) * LANE] for h in range(SPARSE_HEADS)], axis=0).astype(bf16)
    m_scr[...] = jnp.full(m_scr.shape, M_INIT, f32)

    def score_pass(c, n_eq_before):
        s_idx = sc_scr[c]
        is_eq = s_idx == thr
        eq_f = jnp.where(is_eq, 1.0, 0.0)
        rank = n_eq_before + jnp.dot(earlier, eq_f.astype(bf16), preferred_element_type=f32)
        selected = (s_idx > thr) | (is_eq & (rank < need))
        bias = jnp.transpose(jnp.where(selected, 0.0, MASK_BIAS))
        s = _dot_nt(q6, k_ref[key_rows(c), :]) + jnp.concatenate([bias] * SPARSE_HEADS, axis=0)
        s_scr[c] = s
        m_scr[...] = jnp.maximum(m_scr[...], _fold_lanes(jnp.maximum, s))
        return n_eq_before + jnp.sum(_fold_rows(jnp.add, eq_f), axis=0, keepdims=True)

    lax.fori_loop(0, n_chunks, score_pass, jnp.zeros((1, tq), f32))
    m = jnp.max(m_scr[...], axis=1, keepdims=True)
    m_scr[...] = jnp.broadcast_to(m, m_scr.shape)
    l_scr[...] = jnp.zeros(l_scr.shape, f32)
    acc_scr[...] = jnp.zeros(acc_scr.shape, f32)

    def sum_pass(c, _):
        mb = m_scr[...]
        p = jnp.exp2(s_scr[c] - jnp.concatenate([mb] * (kc // LANE), axis=1))
        l_scr[...] += _fold_lanes(jnp.add, p)
        acc_scr[...] += jnp.dot(p.astype(bf16), v_ref[key_rows(c), :], preferred_element_type=f32)
        return 0

    lax.fori_loop(0, n_chunks, sum_pass, 0)
    out = acc_scr[...] / jnp.sum(l_scr[...], axis=1, keepdims=True)
    gate = _silu(g_ref[...].astype(f32))
    for h in range(SPARSE_HEADS):
        cols = slice(h * LANE, (h + 1) * LANE)
        o_ref[:, cols] = (out[h * tq:(h + 1) * tq, :] * gate[:, cols]).astype(o_ref.dtype)


def _dsa_mixer(u, bsz, seq):
    tq = min(256, seq)
    nq = seq // tq
    top_k = min(INDEX_TOPK_MAX, seq // 4)
    wide = SPARSE_WIDTH // LANE
    iq_wide = IDX_HEADS * IDX_DIM // LANE
    rows6 = SPARSE_HEADS * tq

    def q_spec(width_units, unit0):
        assert unit0 % width_units == 0
        return pl.BlockSpec((tq, width_units * LANE), lambda b, i: (b * nq + i, unit0 // width_units))

    def kv_spec(unit):
        return pl.BlockSpec((seq, LANE), lambda b, i: (b, unit))

    return pl.pallas_call(
        functools.partial(_dsa_kernel, top_k=top_k),
        grid=(bsz, nq),
        in_specs=[q_spec(wide, U_SQ), q_spec(wide, U_SG), q_spec(iq_wide, U_IQ), q_spec(1, U_IW),
                  kv_spec(U_IK), kv_spec(U_SK), kv_spec(U_SV)],
        out_specs=pl.BlockSpec((tq, SPARSE_WIDTH), lambda b, i: (b * nq + i, 0)),
        out_shape=jax.ShapeDtypeStruct((bsz * seq, SPARSE_WIDTH), jnp.bfloat16),
        scratch_shapes=[pltpu.VMEM((nq, tq, tq), jnp.float32),
                        pltpu.VMEM((nq, rows6, tq), jnp.float32),
                        pltpu.VMEM((rows6, LANE), jnp.float32),
                        pltpu.VMEM((rows6, LANE), jnp.float32),
                        pltpu.VMEM((rows6, SPARSE_HEAD_DIM), jnp.float32)],
        compiler_params=_params("parallel", "arbitrary"),
        name="dsa_mixer",
    )(u, u, u, u, u, u, u)


def _diff_kernel(q_ref, g_ref, k_ref, v_ref, lq1_ref, lk1_ref, lq2_ref, lk2_ref, sg_ref, o_ref,
                 s_scr, m_scr, l_scr, acc_scr, *, lambda_init):
    tq = q_ref.shape[0]
    kc = tq
    i = pl.program_id(1)
    f32 = jnp.float32
    bf16 = jnp.bfloat16
    scale = DIFF_QK_DIM ** -0.5

    lam = (jnp.exp(jnp.sum(lq1_ref[...] * lk1_ref[...], axis=1, keepdims=True))
           - jnp.exp(jnp.sum(lq2_ref[...] * lk2_ref[...], axis=1, keepdims=True)) + lambda_init)

    lane = lax.broadcasted_iota(jnp.int32, (tq, LANE), 1)
    first = (lane & (DIFF_QK_DIM - 1)) < DIFF_QK_DIM // 2
    q2 = []
    for h in range(DIFF_HEADS):
        q = q_ref[:, h * LANE:(h + 1) * LANE].astype(f32) * (scale * LOG2E)
        q2.append(jnp.concatenate([jnp.where(first, q, 0.0), jnp.where(first, 0.0, q)], axis=0).astype(bf16))
    above_diag = (lax.broadcasted_iota(jnp.int32, (tq, kc), 1) > lax.broadcasted_iota(jnp.int32, (tq, kc), 0))
    diag_bias = jnp.where(above_diag, MASK_BIAS, 0.0)
    diag_bias2 = jnp.concatenate([diag_bias, diag_bias], axis=0)

    def key_rows(c):
        return pl.ds(pl.multiple_of(c * kc, kc), kc)

    def scores(c, h, on_diagonal):
        s = _dot_nt(q2[h], k_ref[key_rows(c), h * LANE:(h + 1) * LANE])
        return s + diag_bias2 if on_diagonal else s

    def max_step(c, on_diagonal):
        for h in range(DIFF_HEADS):
            s = scores(c, h, on_diagonal)
            s_scr[c, h] = s
            m_scr[h] = jnp.maximum(m_scr[h], _fold_lanes(jnp.maximum, s))

    def sum_step(c, on_diagonal):
        del on_diagonal
        for h in range(DIFF_HEADS):
            mb = m_scr[h]
            p = jnp.exp2(s_scr[c, h] - jnp.concatenate([mb] * (kc // LANE), axis=1))
            l_scr[h] += _fold_lanes(jnp.add, p)
            acc_scr[h] += jnp.dot(p.astype(bf16), v_ref[key_rows(c), h * LANE:(h + 1) * LANE],
                                  preferred_element_type=f32)

    def loop(step):
        def body(c, _):
            step(c, False)
            return 0
        lax.fori_loop(0, i, body, 0)
        step(i, True)

    m_scr[...] = jnp.full(m_scr.shape, M_INIT, f32)
    loop(max_step)
    for h in range(DIFF_HEADS):
        m_scr[h] = jnp.broadcast_to(jnp.max(m_scr[h], axis=1, keepdims=True), m_scr.shape[1:])
    l_scr[...] = jnp.zeros(l_scr.shape, f32)
    acc_scr[...] = jnp.zeros(acc_scr.shape, f32)
    loop(sum_step)

    for h in range(DIFF_HEADS):
        cols = slice(h * LANE, (h + 1) * LANE)
        o = acc_scr[h] / jnp.sum(l_scr[h], axis=1, keepdims=True)
        o = o[:tq, :] - lam * o[tq:, :]
        ms = jnp.sum(o * o, axis=1, keepdims=True) * (1.0 / DIFF_V_DIM)
        o = o * lax.rsqrt(ms + SUBLN_EPS) * sg_ref[...] * (1.0 - lambda_init)
        o_ref[:, cols] = (o * _silu(g_ref[:, cols].astype(f32))).astype(o_ref.dtype)


def _diff_mixer(u, lq1, lk1, lq2, lk2, subln_g, lambda_init, bsz, seq):
    tq = min(256, seq)
    nq = seq // tq
    assert U_DQ % DIFF_HEADS == 0 and U_DG % DIFF_HEADS == 0 and U_DK % DIFF_HEADS == 0 and U_DV % DIFF_HEADS == 0

    def q_spec(unit0):
        return pl.BlockSpec((tq, DIFF_WIDTH), lambda b, i: (b * nq + i, unit0 // DIFF_HEADS))

    def kv_spec(unit0):
        return pl.BlockSpec((seq, DIFF_WIDTH), lambda b, i: (b, unit0 // DIFF_HEADS))

    def vec_spec(n):
        return pl.BlockSpec((1, n), lambda b, i: (0, 0))

    return pl.pallas_call(
        functools.partial(_diff_kernel, lambda_init=lambda_init),
        grid=(bsz, nq),
        in_specs=[q_spec(U_DQ), q_spec(U_DG), kv_spec(U_DK), kv_spec(U_DV),
                  vec_spec(DIFF_QK_DIM), vec_spec(DIFF_QK_DIM), vec_spec(DIFF_QK_DIM), vec_spec(DIFF_QK_DIM),
                  vec_spec(DIFF_V_DIM)],
        out_specs=pl.BlockSpec((tq, DIFF_WIDTH), lambda b, i: (b * nq + i, 0)),
        out_shape=jax.ShapeDtypeStruct((bsz * seq, DIFF_WIDTH), jnp.bfloat16),
        scratch_shapes=[pltpu.VMEM((nq, DIFF_HEADS, 2 * tq, tq), jnp.float32),
                        pltpu.VMEM((DIFF_HEADS, 2 * tq, LANE), jnp.float32),
                        pltpu.VMEM((DIFF_HEADS, 2 * tq, LANE), jnp.float32),
                        pltpu.VMEM((DIFF_HEADS, 2 * tq, DIFF_V_DIM), jnp.float32)],
        compiler_params=_params("parallel", "arbitrary"),
        name="diff_mixer",
    )(u, u, u, u, lq1.reshape(1, -1), lk1.reshape(1, -1), lq2.reshape(1, -1), lk2.reshape(1, -1),
      subln_g.reshape(1, -1))


def _outproj_kernel(x_ref, yc_ref, ys_ref, yd_ref, w_ref, g_ref, o_ref, *, final_norm):
    acc = x_ref[...]
    acc = acc + jnp.dot(yc_ref[...], w_ref[0:CONV_WIDTH, :], preferred_element_type=jnp.float32)
    acc = acc + jnp.dot(ys_ref[...], w_ref[CONV_WIDTH:CONV_WIDTH + SPARSE_WIDTH, :],
                        preferred_element_type=jnp.float32)
    acc = acc + jnp.dot(yd_ref[...], w_ref[CONV_WIDTH + SPARSE_WIDTH:, :], preferred_element_type=jnp.float32)
    if final_norm:
        ms = jnp.sum(acc * acc, axis=-1, keepdims=True) * (1.0 / D_MODEL)
        acc = acc * lax.rsqrt(ms + NORM_EPS) * g_ref[...]
    o_ref[...] = acc


def _outproj(x2d, y_conv, y_sparse, y_diff, w_out_bf16, final_g, final_norm):
    m = x2d.shape[0]
    tm = min(512, m)

    def rows(width):
        return pl.BlockSpec((tm, width), lambda i: (i, 0))

    return pl.pallas_call(
        functools.partial(_outproj_kernel, final_norm=final_norm),
        grid=(m // tm,),
        in_specs=[rows(D_MODEL), rows(CONV_WIDTH), rows(SPARSE_WIDTH), rows(DIFF_WIDTH),
                  pl.BlockSpec((D_MODEL, D_MODEL), lambda i: (0, 0)),
                  pl.BlockSpec((1, D_MODEL), lambda i: (0, 0))],
        out_specs=rows(D_MODEL),
        out_shape=jax.ShapeDtypeStruct((m, D_MODEL), jnp.float32),
        compiler_params=_params("parallel"),
        name="outproj",
    )(x2d, y_conv, y_sparse, y_diff, w_out_bf16, final_g.reshape(1, D_MODEL))


def kernel(x, positions, norm_w, w_in, conv_w, lam_q1, lam_k1, lam_q2, lam_k2, subln_w, w_out, final_norm_w):
    bsz, seq, _ = x.shape
    depth = norm_w.shape[0]
    x2d = x.reshape(bsz * seq, D_MODEL)
    tables = _rope_tables(positions)
    w_prep = _prep_w_in(w_in)
    for layer in range(depth):
        lambda_init = 0.8 - 0.6 * math.exp(-0.3 * layer)
        u = _inproj(x2d, norm_w[layer], w_prep, layer, tables)
        y_conv = _conv_mixer(u, conv_w[layer], bsz, seq)
        y_sparse = _dsa_mixer(u, bsz, seq)
        y_diff = _diff_mixer(u, lam_q1[layer], lam_k1[layer], lam_q2[layer], lam_k2[layer], subln_w[layer],
                             lambda_init, bsz, seq)
        x2d = _outproj(x2d, y_conv, y_sparse, y_diff, w_out[layer].astype(jnp.bfloat16), final_norm_w,
                       final_norm=(layer == depth - 1))
    return x2d.reshape(bsz, seq, D_MODEL)
```

```python
import functools
import math

import jax
import jax.numpy as jnp
import numpy as np
from jax import lax
from jax.experimental import pallas as pl
from jax.experimental.pallas import tpu as pltpu

D_MODEL = 2048
CONV_WIDTH = D_MODEL // 4
CONV_K = 3
SPARSE_HEADS = 6
SPARSE_HEAD_DIM = 128
SPARSE_WIDTH = SPARSE_HEADS * SPARSE_HEAD_DIM
IDX_HEADS = 8
IDX_DIM = 64
INDEX_TOPK_MAX = 256
DIFF_HEADS = 6
DIFF_QK_DIM = 64
DIFF_V_DIM = 2 * DIFF_QK_DIM
DIFF_WIDTH = DIFF_HEADS * DIFF_V_DIM
ROPE_THETA = 10000.0
NORM_EPS = 1e-6
SUBLN_EPS = 1e-5

LANE = 128
SUBLANE = 8
VMEM_LIMIT_BYTES = 56 * 1024 * 1024

U_SQ, U_SG, U_DQ, U_DK, U_DV, U_DG = 0, 6, 12, 18, 24, 30
U_AB, U_AC, U_AH, U_AG = 36, 40, 44, 48
U_IQ, U_SK, U_SV, U_IK, U_IW = 52, 56, 57, 58, 59
N_UNITS = 60
N_COLS = N_UNITS * LANE
UNITS_PER_TILE = 4
T_PLAIN, T_ROPE128, T_ROPE64 = 0, 1, 2
UNIT_TYPES = np.zeros((N_UNITS,), np.int32)
UNIT_TYPES[U_SQ:U_SQ + 6] = T_ROPE128
UNIT_TYPES[U_SK] = T_ROPE128
UNIT_TYPES[U_DQ:U_DQ + 6] = T_ROPE64
UNIT_TYPES[U_DK:U_DK + 6] = T_ROPE64
UNIT_TYPES[U_IQ:U_IQ + 4] = T_ROPE64
UNIT_TYPES[U_IK] = T_ROPE64

M_INIT = -1e30
MASK_BIAS = -2e30
LOG2E = math.log2(math.e)

BISECT_STEPS = 14


def _params(*sem):
    return pltpu.CompilerParams(dimension_semantics=sem, vmem_limit_bytes=VMEM_LIMIT_BYTES)


def _silu(x):
    return x / (1.0 + jnp.exp(-x))


def _dot_nt(a, b):
    return lax.dot_general(a, b, (((1,), (1,)), ((), ())), preferred_element_type=jnp.float32)


def _tree(op, parts):
    parts = list(parts)
    while len(parts) > 1:
        nxt = [op(parts[n], parts[n + 1]) for n in range(0, len(parts) - 1, 2)]
        if len(parts) % 2:
            nxt.append(parts[-1])
        parts = nxt
    return parts[0]


def _fold_lanes(op, x):
    return _tree(op, [x[:, t * LANE:(t + 1) * LANE] for t in range(x.shape[1] // LANE)])


def _fold_rows(op, x):
    return _tree(op, [x[g * SUBLANE:(g + 1) * SUBLANE, :] for g in range(x.shape[0] // SUBLANE)])


TAB_ONE, TAB_ZERO, TAB_C128, TAB_S128, TAB_C64, TAB_S64 = range(6)
N_TABLES = 6
ROPE_COEFS = np.array([[TAB_ONE, TAB_ZERO],
                       [TAB_C128, TAB_S128],
                       [TAB_C64, TAB_S64]],
                      np.int32)
ROPE_SHIFT = LANE // 2


def _rope_tables_kernel(pos_ref, freq_ref, tab_ref):
    pos = pos_ref[...].astype(jnp.float32)
    shape = tab_ref.shape[1:]
    first_half = lax.broadcasted_iota(jnp.int32, shape, 1) < ROPE_SHIFT
    tab_ref[TAB_ONE] = jnp.ones(shape, jnp.float32)
    tab_ref[TAB_ZERO] = jnp.zeros(shape, jnp.float32)
    ang = pos * freq_ref[...]
    cos = jnp.cos(ang)
    sin = jnp.sin(ang)
    cos_r = pltpu.roll(cos, ROPE_SHIFT, 1)
    sin_r = pltpu.roll(sin, ROPE_SHIFT, 1)
    tab_ref[TAB_C128] = jnp.where(first_half, cos, cos_r)
    tab_ref[TAB_S128] = jnp.where(first_half, -sin, sin_r)
    tab_ref[TAB_C64] = jnp.where(first_half, cos_r, cos)
    tab_ref[TAB_S64] = jnp.where(first_half, -sin_r, sin)


def _rope_tables(positions):
    m = positions.size
    ts = min(1024, m)
    pos = positions.reshape(m, 1)

    def inv_freq(d):
        half = d // 2
        return jnp.exp(-math.log(ROPE_THETA) * jnp.arange(half, dtype=jnp.float32) * (2.0 / d))

    f64 = inv_freq(IDX_DIM)
    freq = jnp.concatenate([inv_freq(SPARSE_HEAD_DIM), f64, f64]).reshape(1, LANE)
    return pl.pallas_call(
        _rope_tables_kernel,
        grid=(m // ts,),
        in_specs=[pl.BlockSpec((ts, 1), lambda i: (i, 0)), pl.BlockSpec((1, LANE), lambda i: (0, 0))],
        out_specs=pl.BlockSpec((N_TABLES, ts, LANE), lambda i: (0, i, 0)),
        out_shape=jax.ShapeDtypeStruct((N_TABLES, m, LANE), jnp.float32),
        compiler_params=_params("parallel"),
        name="rope_tables",
    )(pos, freq)


def _inproj_kernel(types_ref, coefs_ref, x_ref, g_ref, w_ref, tab_ref, o_ref, h_scr, acc_scr, *,
                   row_chunk, n_col_tiles, n_tiles):
    s = pl.program_id(0)
    tm = x_ref.shape[0]

    @pl.when((s % n_col_tiles == 0) & (s < n_tiles))
    def _():
        for r in range(0, tm, row_chunk):
            x = x_ref[r:r + row_chunk, :]
            ms = jnp.sum(x * x, axis=-1, keepdims=True) * (1.0 / D_MODEL)
            h_scr[r:r + row_chunk, :] = (x * lax.rsqrt(ms + NORM_EPS) * g_ref[...]).astype(jnp.bfloat16)

    @pl.when(s == 0)
    def _():
        acc_scr[1] = jnp.zeros(acc_scr.shape[1:], jnp.float32)

    prev = jnp.maximum(s - 1, 0)
    prev_slot = (s + 1) % 2
    for u in range(UNITS_PER_TILE):
        base = ROPE_COEFS.shape[1] * types_ref[(prev % n_col_tiles) * UNITS_PER_TILE + u]
        a = acc_scr[prev_slot, :, u * LANE:(u + 1) * LANE]
        r = a * tab_ref[coefs_ref[base]] + pltpu.roll(a, ROPE_SHIFT, 1) * tab_ref[coefs_ref[base + 1]]
        o_ref[:, u * LANE:(u + 1) * LANE] = r.astype(o_ref.dtype)

    acc_scr[s % 2] = _dot_nt(h_scr[...], w_ref[...])


def _inproj(x2d, norm_g, w_prep, layer, tables):
    m = x2d.shape[0]
    tm = min(1024, m)
    tn = UNITS_PER_TILE * LANE
    nt = N_COLS // tn
    n_tiles = (m // tm) * nt

    def cur(s):
        return jnp.minimum(s, n_tiles - 1)

    def prev(s):
        return jnp.maximum(s - 1, 0)

    grid_spec = pltpu.PrefetchScalarGridSpec(
        num_scalar_prefetch=2,
        grid=(n_tiles + 1,),
        in_specs=[
            pl.BlockSpec((tm, D_MODEL), lambda s, t, c: (cur(s) // nt, 0)),
            pl.BlockSpec((1, D_MODEL), lambda s, t, c: (0, 0)),
            pl.BlockSpec((None, tn, D_MODEL), lambda s, t, c: (layer, cur(s) % nt, 0)),
            pl.BlockSpec((N_TABLES, tm, LANE), lambda s, t, c: (0, prev(s) // nt, 0)),
        ],
        out_specs=pl.BlockSpec((tm, tn), lambda s, t, c: (prev(s) // nt, prev(s) % nt)),
        scratch_shapes=[pltpu.VMEM((tm, D_MODEL), jnp.bfloat16), pltpu.VMEM((2, tm, tn), jnp.float32)],
    )
    return pl.pallas_call(
        functools.partial(_inproj_kernel, row_chunk=min(256, tm), n_col_tiles=nt, n_tiles=n_tiles),
        grid_spec=grid_spec,
        out_shape=jax.ShapeDtypeStruct((m, N_COLS), jnp.bfloat16),
        compiler_params=_params("arbitrary"),
        name="inproj",
    )(jnp.asarray(UNIT_TYPES), jnp.asarray(ROPE_COEFS.reshape(-1)), x2d, norm_g.reshape(1, D_MODEL), w_prep,
      tables)


def _w_in_copy_plan():
    widths = [CONV_WIDTH, CONV_WIDTH, CONV_WIDTH, CONV_WIDTH, SPARSE_WIDTH, SPARSE_HEAD_DIM, SPARSE_HEAD_DIM,
              IDX_HEADS * IDX_DIM, IDX_DIM, IDX_HEADS, SPARSE_WIDTH, 2 * DIFF_HEADS * DIFF_QK_DIM,
              2 * DIFF_HEADS * DIFF_QK_DIM, DIFF_WIDTH, DIFF_WIDTH]
    names = ["a_b", "a_c", "a_h", "a_g", "s_q", "s_k", "s_v", "i_q", "i_k", "i_w", "s_g", "d_q", "d_k", "d_v", "d_g"]
    src = dict(zip(names, np.cumsum([0] + widths[:-1]).tolist()))
    width = dict(zip(names, widths))
    plan = []
    for name, unit in [("s_q", U_SQ), ("s_g", U_SG), ("d_v", U_DV), ("d_g", U_DG), ("a_b", U_AB), ("a_c", U_AC),
                       ("a_h", U_AH), ("a_g", U_AG), ("s_k", U_SK), ("s_v", U_SV), ("i_w", U_IW)]:
        plan.append((unit * LANE, src[name], width[name]))
    half = IDX_DIM // 2

    def interleave(dst_unit, head_a, head_b):
        for part, head in enumerate((head_a, head_b, head_a, head_b)):
            plan.append((dst_unit * LANE + part * half, head + (part // 2) * half, half))

    for name, unit in [("d_q", U_DQ), ("d_k", U_DK), ("i_q", U_IQ)]:
        for n in range(width[name] // LANE):
            interleave(unit + n, src[name] + n * LANE, src[name] + n * LANE + IDX_DIM)
    interleave(U_IK, src["i_k"], src["i_k"])
    return plan, sum(widths)


def _prep_w_in_kernel(w_ref, o_ref, *, plan):
    cols = w_ref.shape[2]
    pad0 = U_IW * LANE + IDX_HEADS
    o_ref[0, pad0:(U_IW + 1) * LANE, :] = jnp.zeros(((U_IW + 1) * LANE - pad0, cols), o_ref.dtype)
    for dst, src, width in plan:
        o_ref[0, dst:dst + width, :] = w_ref[0, src:src + width, :].astype(o_ref.dtype)


def _prep_w_in(w_in):
    w_t = jnp.swapaxes(w_in, 1, 2)
    depth, n_in, d_in = w_t.shape
    plan, n_src = _w_in_copy_plan()
    assert n_src == n_in
    tc = 256
    return pl.pallas_call(
        functools.partial(_prep_w_in_kernel, plan=plan),
        grid=(depth, d_in // tc),
        in_specs=[pl.BlockSpec((1, n_in, tc), lambda l, c: (l, 0, c))],
        out_specs=pl.BlockSpec((1, N_COLS, tc), lambda l, c: (l, 0, c)),
        out_shape=jax.ShapeDtypeStruct((depth, N_COLS, d_in), jnp.bfloat16),
        compiler_params=_params("parallel", "parallel"),
        name="prep_w_in",
    )(w_t)


def _conv_kernel(b_ref, c_ref, h_ref, g_ref, w_ref, o_ref):
    z = c_ref[...].astype(jnp.float32) * h_ref[...].astype(jnp.float32)
    row = lax.broadcasted_iota(jnp.int32, z.shape, 0)
    w = w_ref[...]
    conv = w[CONV_K - 1:CONV_K, :] * z
    for back in range(1, CONV_K):
        shifted = jnp.where(row >= back, pltpu.roll(z, back, 0), 0.0)
        conv = conv + w[CONV_K - 1 - back:CONV_K - back, :] * shifted
    y = b_ref[...].astype(jnp.float32) * conv * _silu(g_ref[...].astype(jnp.float32))
    o_ref[...] = y.astype(o_ref.dtype)


def _conv_mixer(u, conv_w, bsz, seq):
    n_units = CONV_WIDTH // LANE

    def spec(unit0):
        return pl.BlockSpec((seq, LANE), lambda b, c: (b, unit0 + c))

    return pl.pallas_call(
        _conv_kernel,
        grid=(bsz, n_units),
        in_specs=[spec(U_AB), spec(U_AC), spec(U_AH), spec(U_AG),
                  pl.BlockSpec((CONV_K, LANE), lambda b, c: (0, c))],
        out_specs=pl.BlockSpec((seq, LANE), lambda b, c: (b, c)),
        out_shape=jax.ShapeDtypeStruct((bsz * seq, CONV_WIDTH), jnp.bfloat16),
        compiler_params=_params("parallel", "parallel"),
        name="conv_mixer",
    )(u, u, u, u, conv_w)


def _dsa_kernel(q_ref, g_ref, iq_ref, iw_ref, ki_ref, k_ref, v_ref, o_ref, sc_scr, s_scr, m_scr, l_scr, acc_scr,
                *, top_k):
    tq = q_ref.shape[0]
    kc = tq
    i = pl.program_id(1)
    n_chunks = i + 1
    idx_scale = (IDX_DIM * IDX_HEADS) ** -0.5
    att_scale = SPARSE_HEAD_DIM ** -0.5
    f32 = jnp.float32
    bf16 = jnp.bfloat16
    top_kf = float(top_k)

    def key_rows(c):
        return pl.ds(pl.multiple_of(c * kc, kc), kc)

    lane = lax.broadcasted_iota(jnp.int32, (tq, LANE), 1)
    low_half = (lane & (IDX_DIM - 1)) < IDX_DIM // 2
    q_heads = []
    for unit in range(IDX_HEADS * IDX_DIM // LANE):
        qu = iq_ref[:, unit * LANE:(unit + 1) * LANE].astype(f32)
        q_heads.append(jnp.where(low_half, qu, 0.0).astype(bf16))
        q_heads.append(jnp.where(low_half, 0.0, qu).astype(bf16))
    w_t = jnp.transpose(iw_ref[...].astype(f32)) * idx_scale
    w_rows = [w_t[h:h + 1, :] for h in range(IDX_HEADS)]
    above_diag = (lax.broadcasted_iota(jnp.int32, (kc, tq), 0) > lax.broadcasted_iota(jnp.int32, (kc, tq), 1))

    def index_chunk(c, carry, on_diagonal):
        rmax8, rmin8 = carry
        kic = ki_ref[key_rows(c), :]
        score = w_rows[0] * jnp.maximum(_dot_nt(kic, q_heads[0]), 0.0)
        for h in range(1, IDX_HEADS):
            score = score + w_rows[h] * jnp.maximum(_dot_nt(kic, q_heads[h]), 0.0)
        if on_diagonal:
            hi_part = jnp.where(above_diag, -jnp.inf, score)
            lo_part = jnp.where(above_diag, jnp.inf, score)
        else:
            hi_part = lo_part = score
        sc_scr[c] = hi_part
        return (jnp.maximum(rmax8, _fold_rows(jnp.maximum, hi_part)),
                jnp.minimum(rmin8, _fold_rows(jnp.minimum, lo_part)))

    carry = (jnp.full((SUBLANE, tq), -jnp.inf, f32), jnp.full((SUBLANE, tq), jnp.inf, f32))
    carry = lax.fori_loop(0, i, lambda c, cr: index_chunk(c, cr, False), carry)
    rmax8, rmin8 = index_chunk(i, carry, True)
    rmax = jnp.max(rmax8, axis=0, keepdims=True)
    rmin = jnp.min(rmin8, axis=0, keepdims=True)

    def count(pred):
        def body(c, acc8):
            return acc8 + _fold_rows(jnp.add, jnp.where(pred(sc_scr[c]), 1.0, 0.0))
        acc8 = lax.fori_loop(0, n_chunks, body, jnp.zeros((SUBLANE, tq), f32))
        return jnp.sum(acc8, axis=0, keepdims=True)

    def min_where(pred):
        def body(c, acc8):
            s = sc_scr[c]
            return jnp.minimum(acc8, _fold_rows(jnp.minimum, jnp.where(pred(s), s, jnp.inf)))
        acc8 = lax.fori_loop(0, n_chunks, body, jnp.full((SUBLANE, tq), jnp.inf, f32))
        return jnp.min(acc8, axis=0, keepdims=True)

    def bisect(_, carry):
        lo, hi = carry
        p = 0.5 * (lo + hi)
        enough = count(lambda s: s >= p) >= top_kf
        return jnp.where(enough, p, lo), jnp.where(enough, hi, p)

    lo, _ = lax.fori_loop(0, BISECT_STEPS, bisect, (rmin, rmax))
    thr = min_where(lambda s: s >= lo)
    n_gt = count(lambda s: s > thr)

    def unsettled(carry):
        return jnp.max(carry[1]) >= top_kf

    def walk(carry):
        thr, n_gt = carry
        nxt = min_where(lambda s: s > thr)
        n_nxt = count(lambda s: s > nxt)
        move = n_gt >= top_kf
        return jnp.where(move, nxt, thr), jnp.where(move, n_nxt, n_gt)

    thr, n_gt = lax.while_loop(unsettled, walk, (thr, n_gt))
    need = top_kf - n_gt

    earlier = (lax.broadcasted_iota(jnp.int32, (kc, kc), 1) < lax.broadcasted_iota(jnp.int32, (kc, kc), 0))
    earlier = jnp.where(earlier, 1.0, 0.0).astype(bf16)
    q = q_ref[...].astype(f32) * (att_scale * LOG2E)
    q6 = jnp.concatenate([q[:, h * LANE:(h + 1) * LANE] for h in range(SPARSE_HEADS)], axis=0).astype(bf16)
    m_scr[...] = jnp.full(m_scr.shape, M_INIT, f32)

    def score_pass(c, n_eq_before):
        s_idx = sc_scr[c]
        is_eq = s_idx == thr
        eq_f = jnp.where(is_eq, 1.0, 0.0)
        rank = n_eq_before + jnp.dot(earlier, eq_f.astype(bf16), preferred_element_type=f32)
        selected = (s_idx > thr) | (is_eq & (rank < need))
        bias = jnp.transpose(jnp.where(selected, 0.0, MASK_BIAS))
        s = _dot_nt(q6, k_ref[key_rows(c), :]) + jnp.concatenate([bias] * SPARSE_HEADS, axis=0)
        s_scr[c] = s
        m_scr[...] = jnp.maximum(m_scr[...], _fold_lanes(jnp.maximum, s))
        return n_eq_before + jnp.sum(_fold_rows(jnp.add, eq_f), axis=0, keepdims=True)

    lax.fori_loop(0, n_chunks, score_pass, jnp.zeros((1, tq), f32))
    m = jnp.max(m_scr[...], axis=1, keepdims=True)
    m_scr[...] = jnp.broadcast_to(m, m_scr.shape)
    l_scr[...] = jnp.zeros(l_scr.shape, f32)
    acc_scr[...] = jnp.zeros(acc_scr.shape, f32)

    def sum_pass(c, _):
        mb = m_scr[...]
        p = jnp.exp2(s_scr[c] - jnp.concatenate([mb] * (kc // LANE), axis=1))
        l_scr[...] += _fold_lanes(jnp.add, p)
        acc_scr[...] += jnp.dot(p.astype(bf16), v_ref[key_rows(c), :], preferred_element_type=f32)
        return 0

    lax.fori_loop(0, n_chunks, sum_pass, 0)
    out = acc_scr[...] / jnp.sum(l_scr[...], axis=1, keepdims=True)
    gate = _silu(g_ref[...].astype(f32))
    for h in range(SPARSE_HEADS):
        cols = slice(h * LANE, (h + 1) * LANE)
        o_ref[:, cols] = (out[h * tq:(h + 1) * tq, :] * gate[:, cols]).astype(o_ref.dtype)


def _dsa_mixer(u, bsz, seq):
    tq = min(256, seq)
    nq = seq // tq
    top_k = min(INDEX_TOPK_MAX, seq // 4)
    wide = SPARSE_WIDTH // LANE
    iq_wide = IDX_HEADS * IDX_DIM // LANE
    rows6 = SPARSE_HEADS * tq

    def q_spec(width_units, unit0):
        assert unit0 % width_units == 0
        return pl.BlockSpec((tq, width_units * LANE), lambda b, i: (b * nq + i, unit0 // width_units))

    def kv_spec(unit):
        return pl.BlockSpec((seq, LANE), lambda b, i: (b, unit))

    return pl.pallas_call(
        functools.partial(_dsa_kernel, top_k=top_k),
        grid=(bsz, nq),
        in_specs=[q_spec(wide, U_SQ), q_spec(wide, U_SG), q_spec(iq_wide, U_IQ), q_spec(1, U_IW),
                  kv_spec(U_IK), kv_spec(U_SK), kv_spec(U_SV)],
        out_specs=pl.BlockSpec((tq, SPARSE_WIDTH), lambda b, i: (b * nq + i, 0)),
        out_shape=jax.ShapeDtypeStruct((bsz * seq, SPARSE_WIDTH), jnp.bfloat16),
        scratch_shapes=[pltpu.VMEM((nq, tq, tq), jnp.float32),
                        pltpu.VMEM((nq, rows6, tq), jnp.float32),
                        pltpu.VMEM((rows6, LANE), jnp.float32),
                        pltpu.VMEM((rows6, LANE), jnp.float32),
                        pltpu.VMEM((rows6, SPARSE_HEAD_DIM), jnp.float32)],
        compiler_params=_params("parallel", "arbitrary"),
        name="dsa_mixer",
    )(u, u, u, u, u, u, u)


def _diff_kernel(q_ref, g_ref, k_ref, v_ref, lq1_ref, lk1_ref, lq2_ref, lk2_ref, sg_ref, o_ref,
                 s_scr, m_scr, l_scr, acc_scr, *, lambda_init):
    tq = q_ref.shape[0]
    kc = tq
    i = pl.program_id(1)
    f32 = jnp.float32
    bf16 = jnp.bfloat16
    scale = DIFF_QK_DIM ** -0.5

    lam = (jnp.exp(jnp.sum(lq1_ref[...] * lk1_ref[...], axis=1, keepdims=True))
           - jnp.exp(jnp.sum(lq2_ref[...] * lk2_ref[...], axis=1, keepdims=True)) + lambda_init)

    lane = lax.broadcasted_iota(jnp.int32, (tq, LANE), 1)
    first = (lane & (DIFF_QK_DIM - 1)) < DIFF_QK_DIM // 2
    q2 = []
    for h in range(DIFF_HEADS):
        q = q_ref[:, h * LANE:(h + 1) * LANE].astype(f32) * (scale * LOG2E)
        q2.append(jnp.concatenate([jnp.where(first, q, 0.0), jnp.where(first, 0.0, q)], axis=0).astype(bf16))
    above_diag = (lax.broadcasted_iota(jnp.int32, (tq, kc), 1) > lax.broadcasted_iota(jnp.int32, (tq, kc), 0))
    diag_bias = jnp.where(above_diag, MASK_BIAS, 0.0)
    diag_bias2 = jnp.concatenate([diag_bias, diag_bias], axis=0)

    def key_rows(c):
        return pl.ds(pl.multiple_of(c * kc, kc), kc)

    def scores(c, h, on_diagonal):
        s = _dot_nt(q2[h], k_ref[key_rows(c), h * LANE:(h + 1) * LANE])
        return s + diag_bias2 if on_diagonal else s

    def max_step(c, on_diagonal):
        for h in range(DIFF_HEADS):
            s = scores(c, h, on_diagonal)
            s_scr[c, h] = s
            m_scr[h] = jnp.maximum(m_scr[h], _fold_lanes(jnp.maximum, s))

    def sum_step(c, on_diagonal):
        del on_diagonal
        for h in range(DIFF_HEADS):
            mb = m_scr[h]
            p = jnp.exp2(s_scr[c, h] - jnp.concatenate([mb] * (kc // LANE), axis=1))
            l_scr[h] += _fold_lanes(jnp.add, p)
            acc_scr[h] += jnp.dot(p.astype(bf16), v_ref[key_rows(c), h * LANE:(h + 1) * LANE],
                                  preferred_element_type=f32)

    def loop(step):
        def body(c, _):
            step(c, False)
            return 0
        lax.fori_loop(0, i, body, 0)
        step(i, True)

    m_scr[...] = jnp.full(m_scr.shape, M_INIT, f32)
    loop(max_step)
    for h in range(DIFF_HEADS):
        m_scr[h] = jnp.broadcast_to(jnp.max(m_scr[h], axis=1, keepdims=True), m_scr.shape[1:])
    l_scr[...] = jnp.zeros(l_scr.shape, f32)
    acc_scr[...] = jnp.zeros(acc_scr.shape, f32)
    loop(sum_step)

    for h in range(DIFF_HEADS):
        cols = slice(h * LANE, (h + 1) * LANE)
        o = acc_scr[h] / jnp.sum(l_scr[h], axis=1, keepdims=True)
        o = o[:tq, :] - lam * o[tq:, :]
        ms = jnp.sum(o * o, axis=1, keepdims=True) * (1.0 / DIFF_V_DIM)
        o = o * lax.rsqrt(ms + SUBLN_EPS) * sg_ref[...] * (1.0 - lambda_init)
        o_ref[:, cols] = (o * _silu(g_ref[:, cols].astype(f32))).astype(o_ref.dtype)


def _diff_mixer(u, lq1, lk1, lq2, lk2, subln_g, lambda_init, bsz, seq):
    tq = min(256, seq)
    nq = seq // tq
    assert U_DQ % DIFF_HEADS == 0 and U_DG % DIFF_HEADS == 0 and U_DK % DIFF_HEADS == 0 and U_DV % DIFF_HEADS == 0

    def q_spec(unit0):
        return pl.BlockSpec((tq, DIFF_WIDTH), lambda b, i: (b * nq + i, unit0 // DIFF_HEADS))

    def kv_spec(unit0):
        return pl.BlockSpec((seq, DIFF_WIDTH), lambda b, i: (b, unit0 // DIFF_HEADS))

    def vec_spec(n):
        return pl.BlockSpec((1, n), lambda b, i: (0, 0))

    return pl.pallas_call(
        functools.partial(_diff_kernel, lambda_init=lambda_init),
        grid=(bsz, nq),
        in_specs=[q_spec(U_DQ), q_spec(U_DG), kv_spec(U_DK), kv_spec(U_DV),
                  vec_spec(DIFF_QK_DIM), vec_spec(DIFF_QK_DIM), vec_spec(DIFF_QK_DIM), vec_spec(DIFF_QK_DIM),
                  vec_spec(DIFF_V_DIM)],
        out_specs=pl.BlockSpec((tq, DIFF_WIDTH), lambda b, i: (b * nq + i, 0)),
        out_shape=jax.ShapeDtypeStruct((bsz * seq, DIFF_WIDTH), jnp.bfloat16),
        scratch_shapes=[pltpu.VMEM((nq, DIFF_HEADS, 2 * tq, tq), jnp.float32),
                        pltpu.VMEM((DIFF_HEADS, 2 * tq, LANE), jnp.float32),
                        pltpu.VMEM((DIFF_HEADS, 2 * tq, LANE), jnp.float32),
                        pltpu.VMEM((DIFF_HEADS, 2 * tq, DIFF_V_DIM), jnp.float32)],
        compiler_params=_params("parallel", "arbitrary"),
        name="diff_mixer",
    )(u, u, u, u, lq1.reshape(1, -1), lk1.reshape(1, -1), lq2.reshape(1, -1), lk2.reshape(1, -1),
      subln_g.reshape(1, -1))


def _outproj_kernel(x_ref, yc_ref, ys_ref, yd_ref, w_ref, g_ref, o_ref, *, final_norm):
    acc = x_ref[...]
    acc = acc + jnp.dot(yc_ref[...], w_ref[0:CONV_WIDTH, :], preferred_element_type=jnp.float32)
    acc = acc + jnp.dot(ys_ref[...], w_ref[CONV_WIDTH:CONV_WIDTH + SPARSE_WIDTH, :],
                        preferred_element_type=jnp.float32)
    acc = acc + jnp.dot(yd_ref[...], w_ref[CONV_WIDTH + SPARSE_WIDTH:, :], preferred_element_type=jnp.float32)
    if final_norm:
        ms = jnp.sum(acc * acc, axis=-1, keepdims=True) * (1.0 / D_MODEL)
        acc = acc * lax.rsqrt(ms + NORM_EPS) * g_ref[...]
    o_ref[...] = acc


def _outproj(x2d, y_conv, y_sparse, y_diff, w_out_bf16, final_g, final_norm):
    m = x2d.shape[0]
    tm = min(512, m)

    def rows(width):
        return pl.BlockSpec((tm, width), lambda i: (i, 0))

    return pl.pallas_call(
        functools.partial(_outproj_kernel, final_norm=final_norm),
        grid=(m // tm,),
        in_specs=[rows(D_MODEL), rows(CONV_WIDTH), rows(SPARSE_WIDTH), rows(DIFF_WIDTH),
                  pl.BlockSpec((D_MODEL, D_MODEL), lambda i: (0, 0)),
                  pl.BlockSpec((1, D_MODEL), lambda i: (0, 0))],
        out_specs=rows(D_MODEL),
        out_shape=jax.ShapeDtypeStruct((m, D_MODEL), jnp.float32),
        compiler_params=_params("parallel"),
        name="outproj",
    )(x2d, y_conv, y_sparse, y_diff, w_out_bf16, final_g.reshape(1, D_MODEL))


def kernel(x, positions, norm_w, w_in, conv_w, lam_q1, lam_k1, lam_q2, lam_k2, subln_w, w_out, final_norm_w):
    bsz, seq, _ = x.shape
    depth = norm_w.shape[0]
    x2d = x.reshape(bsz * seq, D_MODEL)
    tables = _rope_tables(positions)
    w_prep = _prep_w_in(w_in)
    for layer in range(depth):
        lambda_init = 0.8 - 0.6 * math.exp(-0.3 * layer)
        u = _inproj(x2d, norm_w[layer], w_prep, layer, tables)
        y_conv = _conv_mixer(u, conv_w[layer], bsz, seq)
        y_sparse = _dsa_mixer(u, bsz, seq)
        y_diff = _diff_mixer(u, lam_q1[layer], lam_k1[layer], lam_q2[layer], lam_k2[layer], subln_w[layer],
                             lambda_init, bsz, seq)
        x2d = _outproj(x2d, y_conv, y_sparse, y_diff, w_out[layer].astype(jnp.bfloat16), final_norm_w,
                       final_norm=(layer == depth - 1))
    return x2d.reshape(bsz, seq, D_MODEL)
```

```python
import functools
import math

import jax
import jax.numpy as jnp
import numpy as np
from jax import lax
from jax.experimental import pallas as pl
from jax.experimental.pallas import tpu as pltpu

D_MODEL = 2048
CONV_WIDTH = D_MODEL // 4
CONV_K = 3
SPARSE_HEADS = 6
SPARSE_HEAD_DIM = 128
SPARSE_WIDTH = SPARSE_HEADS * SPARSE_HEAD_DIM
IDX_HEADS = 8
IDX_DIM = 64
INDEX_TOPK_MAX = 256
DIFF_HEADS = 6
DIFF_QK_DIM = 64
DIFF_V_DIM = 2 * DIFF_QK_DIM
DIFF_WIDTH = DIFF_HEADS * DIFF_V_DIM
ROPE_THETA = 10000.0
NORM_EPS = 1e-6
SUBLN_EPS = 1e-5

LANE = 128
SUBLANE = 8
VMEM_LIMIT_BYTES = 56 * 1024 * 1024

U_SQ, U_SG, U_DQ, U_DK, U_DV, U_DG = 0, 6, 12, 18, 24, 30
U_AB, U_AC, U_AH, U_AG = 36, 40, 44, 48
U_IQ, U_SK, U_SV, U_IK, U_IW = 52, 56, 57, 58, 59
N_UNITS = 60
N_COLS = N_UNITS * LANE
UNITS_PER_TILE = 4
T_PLAIN, T_ROPE128, T_ROPE64 = 0, 1, 2
UNIT_TYPES = np.zeros((N_UNITS,), np.int32)
UNIT_TYPES[U_SQ:U_SQ + 6] = T_ROPE128
UNIT_TYPES[U_SK] = T_ROPE128
UNIT_TYPES[U_DQ:U_DQ + 6] = T_ROPE64
UNIT_TYPES[U_DK:U_DK + 6] = T_ROPE64
UNIT_TYPES[U_IQ:U_IQ + 4] = T_ROPE64
UNIT_TYPES[U_IK] = T_ROPE64

M_INIT = -1e30
MASK_BIAS = -2e30
LOG2E = math.log2(math.e)

BISECT_STEPS = 14


def _params(*sem):
    return pltpu.CompilerParams(dimension_semantics=sem, vmem_limit_bytes=VMEM_LIMIT_BYTES)


def _silu(x):
    return x / (1.0 + jnp.exp(-x))


def _dot_nt(a, b):
    return lax.dot_general(a, b, (((1,), (1,)), ((), ())), preferred_element_type=jnp.float32)


def _tree(op, parts):
    parts = list(parts)
    while len(parts) > 1:
        nxt = [op(parts[n], parts[n + 1]) for n in range(0, len(parts) - 1, 2)]
        if len(parts) % 2:
            nxt.append(parts[-1])
        parts = nxt
    return parts[0]


def _chunk_loop(n, step, carry):
    def pair(j, cr):
        return step(2 * j + 1, step(2 * j, cr))
    carry = lax.fori_loop(0, lax.shift_right_logical(n, 1), pair, carry)
    return lax.cond((n & 1) == 1, lambda cr: step(n - 1, cr), lambda cr: cr, carry)


def _fold_lanes(op, x):
    return _tree(op, [x[:, t * LANE:(t + 1) * LANE] for t in range(x.shape[1] // LANE)])


def _fold_rows(op, x):
    return _tree(op, [x[g * SUBLANE:(g + 1) * SUBLANE, :] for g in range(x.shape[0] // SUBLANE)])


TAB_ONE, TAB_ZERO, TAB_C128, TAB_S128, TAB_C64, TAB_S64 = range(6)
N_TABLES = 6
ROPE_COEFS = np.array([[TAB_ONE, TAB_ZERO],
                       [TAB_C128, TAB_S128],
                       [TAB_C64, TAB_S64]],
                      np.int32)
ROPE_SHIFT = LANE // 2


def _rope_tables_kernel(pos_ref, freq_ref, tab_ref):
    pos = pos_ref[...].astype(jnp.float32)
    shape = tab_ref.shape[1:]
    first_half = lax.broadcasted_iota(jnp.int32, shape, 1) < ROPE_SHIFT
    tab_ref[TAB_ONE] = jnp.ones(shape, jnp.float32)
    tab_ref[TAB_ZERO] = jnp.zeros(shape, jnp.float32)
    ang = pos * freq_ref[...]
    cos = jnp.cos(ang)
    sin = jnp.sin(ang)
    cos_r = pltpu.roll(cos, ROPE_SHIFT, 1)
    sin_r = pltpu.roll(sin, ROPE_SHIFT, 1)
    tab_ref[TAB_C128] = jnp.where(first_half, cos, cos_r)
    tab_ref[TAB_S128] = jnp.where(first_half, -sin, sin_r)
    tab_ref[TAB_C64] = jnp.where(first_half, cos_r, cos)
    tab_ref[TAB_S64] = jnp.where(first_half, -sin_r, sin)


def _rope_tables(positions):
    m = positions.size
    ts = min(1024, m)
    pos = positions.reshape(m, 1)

    def inv_freq(d):
        half = d // 2
        return jnp.exp(-math.log(ROPE_THETA) * jnp.arange(half, dtype=jnp.float32) * (2.0 / d))

    f64 = inv_freq(IDX_DIM)
    freq = jnp.concatenate([inv_freq(SPARSE_HEAD_DIM), f64, f64]).reshape(1, LANE)
    return pl.pallas_call(
        _rope_tables_kernel,
        grid=(m // ts,),
        in_specs=[pl.BlockSpec((ts, 1), lambda i: (i, 0)), pl.BlockSpec((1, LANE), lambda i: (0, 0))],
        out_specs=pl.BlockSpec((N_TABLES, ts, LANE), lambda i: (0, i, 0)),
        out_shape=jax.ShapeDtypeStruct((N_TABLES, m, LANE), jnp.float32),
        compiler_params=_params("parallel"),
        name="rope_tables",
    )(pos, freq)


def _inproj_kernel(types_ref, coefs_ref, x_ref, g_ref, w_ref, tab_ref, o_ref, h_scr, acc_scr, *,
                   row_chunk, n_col_tiles, n_tiles):
    s = pl.program_id(0)
    tm = x_ref.shape[0]

    @pl.when((s % n_col_tiles == 0) & (s < n_tiles))
    def _():
        for r in range(0, tm, row_chunk):
            x = x_ref[r:r + row_chunk, :]
            ms = jnp.sum(x * x, axis=-1, keepdims=True) * (1.0 / D_MODEL)
            h_scr[r:r + row_chunk, :] = (x * lax.rsqrt(ms + NORM_EPS) * g_ref[...]).astype(jnp.bfloat16)

    @pl.when(s == 0)
    def _():
        acc_scr[1] = jnp.zeros(acc_scr.shape[1:], jnp.float32)

    prev = jnp.maximum(s - 1, 0)
    prev_slot = (s + 1) % 2
    for u in range(UNITS_PER_TILE):
        base = ROPE_COEFS.shape[1] * types_ref[(prev % n_col_tiles) * UNITS_PER_TILE + u]
        a = acc_scr[prev_slot, :, u * LANE:(u + 1) * LANE]
        r = a * tab_ref[coefs_ref[base]] + pltpu.roll(a, ROPE_SHIFT, 1) * tab_ref[coefs_ref[base + 1]]
        o_ref[:, u * LANE:(u + 1) * LANE] = r.astype(o_ref.dtype)

    acc_scr[s % 2] = jnp.dot(h_scr[...], w_ref[...], preferred_element_type=jnp.float32)


def _inproj(x2d, norm_g, w_prep, layer, tables):
    m = x2d.shape[0]
    tm = min(1024, m)
    tn = UNITS_PER_TILE * LANE
    nt = N_COLS // tn
    n_tiles = (m // tm) * nt

    def cur(s):
        return jnp.minimum(s, n_tiles - 1)

    def prev(s):
        return jnp.maximum(s - 1, 0)

    grid_spec = pltpu.PrefetchScalarGridSpec(
        num_scalar_prefetch=2,
        grid=(n_tiles + 1,),
        in_specs=[
            pl.BlockSpec((tm, D_MODEL), lambda s, t, c: (cur(s) // nt, 0)),
            pl.BlockSpec((1, D_MODEL), lambda s, t, c: (0, 0)),
            pl.BlockSpec((None, D_MODEL, tn), lambda s, t, c: (layer, 0, cur(s) % nt)),
            pl.BlockSpec((N_TABLES, tm, LANE), lambda s, t, c: (0, prev(s) // nt, 0)),
        ],
        out_specs=pl.BlockSpec((tm, tn), lambda s, t, c: (prev(s) // nt, prev(s) % nt)),
        scratch_shapes=[pltpu.VMEM((tm, D_MODEL), jnp.bfloat16), pltpu.VMEM((2, tm, tn), jnp.float32)],
    )
    return pl.pallas_call(
        functools.partial(_inproj_kernel, row_chunk=min(256, tm), n_col_tiles=nt, n_tiles=n_tiles),
        grid_spec=grid_spec,
        out_shape=jax.ShapeDtypeStruct((m, N_COLS), jnp.bfloat16),
        compiler_params=_params("arbitrary"),
        name="inproj",
    )(jnp.asarray(UNIT_TYPES), jnp.asarray(ROPE_COEFS.reshape(-1)), x2d, norm_g.reshape(1, D_MODEL), w_prep,
      tables)


def _w_in_copy_plan():
    widths = [CONV_WIDTH, CONV_WIDTH, CONV_WIDTH, CONV_WIDTH, SPARSE_WIDTH, SPARSE_HEAD_DIM, SPARSE_HEAD_DIM,
              IDX_HEADS * IDX_DIM, IDX_DIM, IDX_HEADS, SPARSE_WIDTH, 2 * DIFF_HEADS * DIFF_QK_DIM,
              2 * DIFF_HEADS * DIFF_QK_DIM, DIFF_WIDTH, DIFF_WIDTH]
    names = ["a_b", "a_c", "a_h", "a_g", "s_q", "s_k", "s_v", "i_q", "i_k", "i_w", "s_g", "d_q", "d_k", "d_v", "d_g"]
    src = dict(zip(names, np.cumsum([0] + widths[:-1]).tolist()))
    width = dict(zip(names, widths))
    plan = []
    for name, unit in [("s_q", U_SQ), ("s_g", U_SG), ("d_v", U_DV), ("d_g", U_DG), ("a_b", U_AB), ("a_c", U_AC),
                       ("a_h", U_AH), ("a_g", U_AG), ("s_k", U_SK), ("s_v", U_SV), ("i_w", U_IW)]:
        plan.append((unit * LANE, src[name], width[name]))
    half = IDX_DIM // 2

    def interleave(dst_unit, head_a, head_b):
        for part, head in enumerate((head_a, head_b, head_a, head_b)):
            plan.append((dst_unit * LANE + part * half, head + (part // 2) * half, half))

    for name, unit in [("d_q", U_DQ), ("d_k", U_DK), ("i_q", U_IQ)]:
        for n in range(width[name] // LANE):
            interleave(unit + n, src[name] + n * LANE, src[name] + n * LANE + IDX_DIM)
    interleave(U_IK, src["i_k"], src["i_k"])
    return plan, sum(widths)


def _prep_w_in_kernel(w_ref, o_ref, rows_scr, *, plan):
    cols = w_ref.shape[2]
    pad0 = U_IW * LANE + IDX_HEADS
    rows_scr[pad0:(U_IW + 1) * LANE, :] = jnp.zeros(((U_IW + 1) * LANE - pad0, cols), rows_scr.dtype)
    for dst, src, width in plan:
        rows_scr[dst:dst + width, :] = w_ref[0, src:src + width, :]
    for unit in range(N_UNITS):
        block = rows_scr[unit * LANE:(unit + 1) * LANE, :]
        o_ref[0, :, unit * LANE:(unit + 1) * LANE] = jnp.transpose(block).astype(o_ref.dtype)


def _prep_w_in(w_in):
    w_t = jnp.swapaxes(w_in, 1, 2)
    depth, n_in, d_in = w_t.shape
    plan, n_src = _w_in_copy_plan()
    assert n_src == n_in
    tc = 256
    return pl.pallas_call(
        functools.partial(_prep_w_in_kernel, plan=plan),
        grid=(depth, d_in // tc),
        in_specs=[pl.BlockSpec((1, n_in, tc), lambda l, c: (l, 0, c))],
        out_specs=pl.BlockSpec((1, tc, N_COLS), lambda l, c: (l, c, 0)),
        out_shape=jax.ShapeDtypeStruct((depth, d_in, N_COLS), jnp.bfloat16),
        scratch_shapes=[pltpu.VMEM((N_COLS, tc), jnp.float32)],
        compiler_params=_params("parallel", "parallel"),
        name="prep_w_in",
    )(w_t)


def _conv_kernel(b_ref, c_ref, h_ref, g_ref, w_ref, o_ref):
    row = lax.broadcasted_iota(jnp.int32, (b_ref.shape[0], LANE), 0)
    for unit in range(CONV_WIDTH // LANE):
        cols = slice(unit * LANE, (unit + 1) * LANE)
        z = c_ref[:, cols].astype(jnp.float32) * h_ref[:, cols].astype(jnp.float32)
        w = w_ref[:, cols]
        conv = w[CONV_K - 1:CONV_K, :] * z
        for back in range(1, CONV_K):
            shifted = jnp.where(row >= back, pltpu.roll(z, back, 0), 0.0)
            conv = conv + w[CONV_K - 1 - back:CONV_K - back, :] * shifted
        y = b_ref[:, cols].astype(jnp.float32) * conv * _silu(g_ref[:, cols].astype(jnp.float32))
        o_ref[:, cols] = y.astype(o_ref.dtype)


def _conv_mixer(u, conv_w, bsz, seq):
    n_units = CONV_WIDTH // LANE

    def spec(unit0):
        assert unit0 % n_units == 0
        return pl.BlockSpec((seq, CONV_WIDTH), lambda b: (b, unit0 // n_units))

    return pl.pallas_call(
        _conv_kernel,
        grid=(bsz,),
        in_specs=[spec(U_AB), spec(U_AC), spec(U_AH), spec(U_AG),
                  pl.BlockSpec((CONV_K, CONV_WIDTH), lambda b: (0, 0))],
        out_specs=pl.BlockSpec((seq, CONV_WIDTH), lambda b: (b, 0)),
        out_shape=jax.ShapeDtypeStruct((bsz * seq, CONV_WIDTH), jnp.bfloat16),
        compiler_params=_params("parallel"),
        name="conv_mixer",
    )(u, u, u, u, conv_w)


def _dsa_kernel(q_ref, g_ref, iq_ref, iw_ref, ki_ref, k_ref, v_ref, o_ref, sc_scr, s_scr, m_scr, acc_scr,
                *, top_k):
    tq = q_ref.shape[0]
    kc = tq
    i = pl.program_id(1)
    n_chunks = i + 1
    idx_scale = (IDX_DIM * IDX_HEADS) ** -0.5
    att_scale = SPARSE_HEAD_DIM ** -0.5
    f32 = jnp.float32
    bf16 = jnp.bfloat16
    top_kf = float(top_k)

    def key_rows(c):
        return pl.ds(pl.multiple_of(c * kc, kc), kc)

    lane = lax.broadcasted_iota(jnp.int32, (tq, LANE), 1)
    low_half = (lane & (IDX_DIM - 1)) < IDX_DIM // 2
    q_heads = []
    for unit in range(IDX_HEADS * IDX_DIM // LANE):
        qu = iq_ref[:, unit * LANE:(unit + 1) * LANE].astype(f32)
        q_heads.append(jnp.where(low_half, qu, 0.0).astype(bf16))
        q_heads.append(jnp.where(low_half, 0.0, qu).astype(bf16))
    w_t = jnp.transpose(iw_ref[...].astype(f32)) * idx_scale
    w_rows = [w_t[h:h + 1, :] for h in range(IDX_HEADS)]
    above_diag = (lax.broadcasted_iota(jnp.int32, (kc, tq), 0) > lax.broadcasted_iota(jnp.int32, (kc, tq), 1))

    def index_chunk(c, carry, on_diagonal):
        rmax8, rmin8 = carry
        kic = ki_ref[key_rows(c), :]
        score = w_rows[0] * jnp.maximum(_dot_nt(kic, q_heads[0]), 0.0)
        for h in range(1, IDX_HEADS):
            score = score + w_rows[h] * jnp.maximum(_dot_nt(kic, q_heads[h]), 0.0)
        if on_diagonal:
            hi_part = jnp.where(above_diag, -jnp.inf, score)
            lo_part = jnp.where(above_diag, jnp.inf, score)
        else:
            hi_part = lo_part = score
        sc_scr[c] = hi_part
        return (jnp.maximum(rmax8, _fold_rows(jnp.maximum, hi_part)),
                jnp.minimum(rmin8, _fold_rows(jnp.minimum, lo_part)))

    carry = (jnp.full((SUBLANE, tq), -jnp.inf, f32), jnp.full((SUBLANE, tq), jnp.inf, f32))
    carry = _chunk_loop(i, lambda c, cr: index_chunk(c, cr, False), carry)
    rmax8, rmin8 = index_chunk(i, carry, True)
    rmax = jnp.max(rmax8, axis=0, keepdims=True)
    rmin = jnp.min(rmin8, axis=0, keepdims=True)

    def count(pred):
        def body(c, acc8):
            return acc8 + _fold_rows(jnp.add, jnp.where(pred(sc_scr[c]), 1.0, 0.0))
        acc8 = lax.fori_loop(0, n_chunks, body, jnp.zeros((SUBLANE, tq), f32))
        return jnp.sum(acc8, axis=0, keepdims=True)

    def min_where(pred):
        def body(c, acc8):
            s = sc_scr[c]
            return jnp.minimum(acc8, _fold_rows(jnp.minimum, jnp.where(pred(s), s, jnp.inf)))
        acc8 = lax.fori_loop(0, n_chunks, body, jnp.full((SUBLANE, tq), jnp.inf, f32))
        return jnp.min(acc8, axis=0, keepdims=True)

    def bisect(_, carry):
        lo, hi = carry
        p = 0.5 * (lo + hi)
        enough = count(lambda s: s >= p) >= top_kf
        return jnp.where(enough, p, lo), jnp.where(enough, hi, p)

    lo, _ = lax.fori_loop(0, BISECT_STEPS, bisect, (rmin, rmax))
    thr = min_where(lambda s: s >= lo)
    n_gt = count(lambda s: s > thr)

    def unsettled(carry):
        return jnp.max(carry[1]) >= top_kf

    def walk(carry):
        thr, n_gt = carry
        nxt = min_where(lambda s: s > thr)
        n_nxt = count(lambda s: s > nxt)
        move = n_gt >= top_kf
        return jnp.where(move, nxt, thr), jnp.where(move, n_nxt, n_gt)

    thr, n_gt = lax.while_loop(unsettled, walk, (thr, n_gt))
    need = top_kf - n_gt

    earlier = (lax.broadcasted_iota(jnp.int32, (kc, kc), 1) < lax.broadcasted_iota(jnp.int32, (kc, kc), 0))
    earlier = jnp.where(earlier, 1.0, 0.0).astype(bf16)
    q = q_ref[...].astype(f32) * (att_scale * LOG2E)
    q6 = jnp.concatenate([q[:, h * LANE:(h + 1) * LANE] for h in range(SPARSE_HEADS)], axis=0).astype(bf16)
    m_scr[...] = jnp.full(m_scr.shape, M_INIT, f32)

    def score_pass(c, n_eq_before):
        s_idx = sc_scr[c]
        is_eq = s_idx == thr
        eq_f = jnp.where(is_eq, 1.0, 0.0)
        rank = n_eq_before + jnp.dot(earlier, eq_f.astype(bf16), preferred_element_type=f32)
        selected = (s_idx > thr) | (is_eq & (rank < need))
        bias = jnp.transpose(jnp.where(selected, 0.0, MASK_BIAS))
        s = _dot_nt(q6, k_ref[key_rows(c), :]) + jnp.concatenate([bias] * SPARSE_HEADS, axis=0)
        s_scr[c] = s
        m_scr[...] = jnp.maximum(m_scr[...], _fold_lanes(jnp.maximum, s))
        return n_eq_before + jnp.sum(_fold_rows(jnp.add, eq_f), axis=0, keepdims=True)

    _chunk_loop(n_chunks, score_pass, jnp.zeros((1, tq), f32))
    m = jnp.max(m_scr[...], axis=1, keepdims=True)
    m_scr[...] = jnp.broadcast_to(m, m_scr.shape)
    acc_scr[...] = jnp.zeros(acc_scr.shape, f32)
    ones_cols = jnp.ones((kc, LANE), bf16)

    def sum_pass(c, _):
        mb = m_scr[...]
        p = jnp.exp2((s_scr[c] - jnp.concatenate([mb] * (kc // LANE), axis=1)).astype(bf16))
        v1 = jnp.concatenate([v_ref[key_rows(c), :], ones_cols], axis=1)
        acc_scr[...] += jnp.dot(p, v1, preferred_element_type=f32)
        return jnp.int32(0)

    _chunk_loop(n_chunks, sum_pass, jnp.int32(0))
    out = acc_scr[:, :SPARSE_HEAD_DIM] / acc_scr[:, SPARSE_HEAD_DIM:]
    gate = _silu(g_ref[...].astype(f32))
    for h in range(SPARSE_HEADS):
        cols = slice(h * LANE, (h + 1) * LANE)
        o_ref[:, cols] = (out[h * tq:(h + 1) * tq, :] * gate[:, cols]).astype(o_ref.dtype)


def _dsa_mixer(u, bsz, seq):
    tq = min(256, seq)
    nq = seq // tq
    top_k = min(INDEX_TOPK_MAX, seq // 4)
    wide = SPARSE_WIDTH // LANE
    iq_wide = IDX_HEADS * IDX_DIM // LANE
    rows6 = SPARSE_HEADS * tq

    def q_spec(width_units, unit0):
        assert unit0 % width_units == 0
        return pl.BlockSpec((tq, width_units * LANE), lambda b, i: (b * nq + i, unit0 // width_units))

    def kv_spec(unit):
        return pl.BlockSpec((seq, LANE), lambda b, i: (b, unit))

    return pl.pallas_call(
        functools.partial(_dsa_kernel, top_k=top_k),
        grid=(bsz, nq),
        in_specs=[q_spec(wide, U_SQ), q_spec(wide, U_SG), q_spec(iq_wide, U_IQ), q_spec(1, U_IW),
                  kv_spec(U_IK), kv_spec(U_SK), kv_spec(U_SV)],
        out_specs=pl.BlockSpec((tq, SPARSE_WIDTH), lambda b, i: (b * nq + i, 0)),
        out_shape=jax.ShapeDtypeStruct((bsz * seq, SPARSE_WIDTH), jnp.bfloat16),
        scratch_shapes=[pltpu.VMEM((nq, tq, tq), jnp.float32),
                        pltpu.VMEM((nq, rows6, tq), jnp.float32),
                        pltpu.VMEM((rows6, LANE), jnp.float32),
                        pltpu.VMEM((rows6, SPARSE_HEAD_DIM + LANE), jnp.float32)],
        compiler_params=_params("parallel", "arbitrary"),
        name="dsa_mixer",
    )(u, u, u, u, u, u, u)


def _diff_kernel(q_ref, g_ref, k_ref, v_ref, lq1_ref, lk1_ref, lq2_ref, lk2_ref, sg_ref, o_ref,
                 s_scr, m_scr, acc_scr, *, lambda_init):
    tq = q_ref.shape[0]
    kc = tq
    i = pl.program_id(1)
    f32 = jnp.float32
    bf16 = jnp.bfloat16
    scale = DIFF_QK_DIM ** -0.5

    lam = (jnp.exp(jnp.sum(lq1_ref[...] * lk1_ref[...], axis=1, keepdims=True))
           - jnp.exp(jnp.sum(lq2_ref[...] * lk2_ref[...], axis=1, keepdims=True)) + lambda_init)

    lane = lax.broadcasted_iota(jnp.int32, (tq, LANE), 1)
    first = (lane & (DIFF_QK_DIM - 1)) < DIFF_QK_DIM // 2
    q2 = []
    for h in range(DIFF_HEADS):
        q = q_ref[:, h * LANE:(h + 1) * LANE].astype(f32) * (scale * LOG2E)
        q2.append(jnp.concatenate([jnp.where(first, q, 0.0), jnp.where(first, 0.0, q)], axis=0).astype(bf16))
    above_diag = (lax.broadcasted_iota(jnp.int32, (tq, kc), 1) > lax.broadcasted_iota(jnp.int32, (tq, kc), 0))
    diag_bias = jnp.where(above_diag, MASK_BIAS, 0.0)
    diag_bias2 = jnp.concatenate([diag_bias, diag_bias], axis=0)
    ones_cols = jnp.ones((kc, LANE), bf16)

    def key_rows(c):
        return pl.ds(pl.multiple_of(c * kc, kc), kc)

    def scores(c, h, on_diagonal):
        s = _dot_nt(q2[h], k_ref[key_rows(c), h * LANE:(h + 1) * LANE])
        return s + diag_bias2 if on_diagonal else s

    def max_step(c, on_diagonal):
        for h in range(DIFF_HEADS):
            s = scores(c, h, on_diagonal)
            s_scr[c, h] = s
            m_scr[h] = jnp.maximum(m_scr[h], _fold_lanes(jnp.maximum, s))

    def sum_step(c, on_diagonal):
        del on_diagonal
        for h in range(DIFF_HEADS):
            mb = m_scr[h]
            p = jnp.exp2((s_scr[c, h] - jnp.concatenate([mb] * (kc // LANE), axis=1)).astype(bf16))
            v1 = jnp.concatenate([v_ref[key_rows(c), h * LANE:(h + 1) * LANE], ones_cols], axis=1)
            acc_scr[h] += jnp.dot(p, v1, preferred_element_type=f32)

    def loop(step):
        def body(c, _):
            step(c, False)
            return jnp.int32(0)
        _chunk_loop(i, body, jnp.int32(0))
        step(i, True)

    m_scr[...] = jnp.full(m_scr.shape, M_INIT, f32)
    loop(max_step)
    for h in range(DIFF_HEADS):
        m_scr[h] = jnp.broadcast_to(jnp.max(m_scr[h], axis=1, keepdims=True), m_scr.shape[1:])
    acc_scr[...] = jnp.zeros(acc_scr.shape, f32)
    loop(sum_step)

    for h in range(DIFF_HEADS):
        cols = slice(h * LANE, (h + 1) * LANE)
        o = acc_scr[h, :, :DIFF_V_DIM] / acc_scr[h, :, DIFF_V_DIM:]
        o = o[:tq, :] - lam * o[tq:, :]
        ms = jnp.sum(o * o, axis=1, keepdims=True) * (1.0 / DIFF_V_DIM)
        o = o * lax.rsqrt(ms + SUBLN_EPS) * sg_ref[...] * (1.0 - lambda_init)
        o_ref[:, cols] = (o * _silu(g_ref[:, cols].astype(f32))).astype(o_ref.dtype)


def _diff_mixer(u, lq1, lk1, lq2, lk2, subln_g, lambda_init, bsz, seq):
    tq = min(256, seq)
    nq = seq // tq
    assert U_DQ % DIFF_HEADS == 0 and U_DG % DIFF_HEADS == 0 and U_DK % DIFF_HEADS == 0 and U_DV % DIFF_HEADS == 0

    def q_spec(unit0):
        return pl.BlockSpec((tq, DIFF_WIDTH), lambda b, i: (b * nq + i, unit0 // DIFF_HEADS))

    def kv_spec(unit0):
        return pl.BlockSpec((seq, DIFF_WIDTH), lambda b, i: (b, unit0 // DIFF_HEADS))

    def vec_spec(n):
        return pl.BlockSpec((1, n), lambda b, i: (0, 0))

    return pl.pallas_call(
        functools.partial(_diff_kernel, lambda_init=lambda_init),
        grid=(bsz, nq),
        in_specs=[q_spec(U_DQ), q_spec(U_DG), kv_spec(U_DK), kv_spec(U_DV),
                  vec_spec(DIFF_QK_DIM), vec_spec(DIFF_QK_DIM), vec_spec(DIFF_QK_DIM), vec_spec(DIFF_QK_DIM),
                  vec_spec(DIFF_V_DIM)],
        out_specs=pl.BlockSpec((tq, DIFF_WIDTH), lambda b, i: (b * nq + i, 0)),
        out_shape=jax.ShapeDtypeStruct((bsz * seq, DIFF_WIDTH), jnp.bfloat16),
        scratch_shapes=[pltpu.VMEM((nq, DIFF_HEADS, 2 * tq, tq), jnp.float32),
                        pltpu.VMEM((DIFF_HEADS, 2 * tq, LANE), jnp.float32),
                        pltpu.VMEM((DIFF_HEADS, 2 * tq, DIFF_V_DIM + LANE), jnp.float32)],
        compiler_params=_params("parallel", "arbitrary"),
        name="diff_mixer",
    )(u, u, u, u, lq1.reshape(1, -1), lk1.reshape(1, -1), lq2.reshape(1, -1), lk2.reshape(1, -1),
      subln_g.reshape(1, -1))


def _outproj_kernel(x_ref, yc_ref, ys_ref, yd_ref, w_ref, g_ref, o_ref, *, final_norm):
    acc = x_ref[...]
    acc = acc + jnp.dot(yc_ref[...], w_ref[0:CONV_WIDTH, :], preferred_element_type=jnp.float32)
    acc = acc + jnp.dot(ys_ref[...], w_ref[CONV_WIDTH:CONV_WIDTH + SPARSE_WIDTH, :],
                        preferred_element_type=jnp.float32)
    acc = acc + jnp.dot(yd_ref[...], w_ref[CONV_WIDTH + SPARSE_WIDTH:, :], preferred_element_type=jnp.float32)
    if final_norm:
        ms = jnp.sum(acc * acc, axis=-1, keepdims=True) * (1.0 / D_MODEL)
        acc = acc * lax.rsqrt(ms + NORM_EPS) * g_ref[...]
    o_ref[...] = acc


def _outproj(x2d, y_conv, y_sparse, y_diff, w_out_bf16, final_g, final_norm):
    m = x2d.shape[0]
    tm = min(512, m)

    def rows(width):
        return pl.BlockSpec((tm, width), lambda i: (i, 0))

    return pl.pallas_call(
        functools.partial(_outproj_kernel, final_norm=final_norm),
        grid=(m // tm,),
        in_specs=[rows(D_MODEL), rows(CONV_WIDTH), rows(SPARSE_WIDTH), rows(DIFF_WIDTH),
                  pl.BlockSpec((D_MODEL, D_MODEL), lambda i: (0, 0)),
                  pl.BlockSpec((1, D_MODEL), lambda i: (0, 0))],
        out_specs=rows(D_MODEL),
        out_shape=jax.ShapeDtypeStruct((m, D_MODEL), jnp.float32),
        compiler_params=_params("parallel"),
        name="outproj",
    )(x2d, y_conv, y_sparse, y_diff, w_out_bf16, final_g.reshape(1, D_MODEL))


def kernel(x, positions, norm_w, w_in, conv_w, lam_q1, lam_k1, lam_q2, lam_k2, subln_w, w_out, final_norm_w):
    bsz, seq, _ = x.shape
    depth = norm_w.shape[0]
    x2d = x.reshape(bsz * seq, D_MODEL)
    tables = _rope_tables(positions)
    w_prep = _prep_w_in(w_in)
    for layer in range(depth):
        lambda_init = 0.8 - 0.6 * math.exp(-0.3 * layer)
        u = _inproj(x2d, norm_w[layer], w_prep, layer, tables)
        y_conv = _conv_mixer(u, conv_w[layer], bsz, seq)
        y_sparse = _dsa_mixer(u, bsz, seq)
        y_diff = _diff_mixer(u, lam_q1[layer], lam_k1[layer], lam_q2[layer], lam_k2[layer], subln_w[layer],
                             lambda_init, bsz, seq)
        x2d = _outproj(x2d, y_conv, y_sparse, y_diff, w_out[layer].astype(jnp.bfloat16), final_norm_w,
                       final_norm=(layer == depth - 1))
    return x2d.reshape(bsz, seq, D_MODEL)
```
